```python
import math
import jax, jax.numpy as jnp
from jax import lax
import numpy as np

D_MODEL = 1024
BATCH = 8
SEQ = 4096
DEPTH = 1

CHUNK = 64
Q_BLOCK = 128
SPARSE_Q_BLOCK = 64
D_MIX = D_MODEL

A_HEADS = 4
A_QK_DIM = 64
A_V_DIM = 2 * A_QK_DIM
A_WIDTH = A_HEADS * A_V_DIM
A_QK_WIDTH = A_HEADS * 2 * A_QK_DIM

B_HEADS = 8
B_HEAD_DIM = 64
B_WIDTH = B_HEADS * B_HEAD_DIM
IDX_HEADS = 4
IDX_DIM = 64
TOPK_MAX = 256

N_BUCKETS = 32
MAX_DISTANCE = 128
N_BIAS_HEADS = A_HEADS + B_HEADS

EPS = 1e-6

SPLIT_SIZES = (A_QK_WIDTH, A_QK_WIDTH, A_WIDTH, A_WIDTH,
               B_WIDTH, B_WIDTH, B_WIDTH, B_WIDTH,
               IDX_HEADS * IDX_DIM, IDX_DIM, IDX_HEADS)
D_IN_PROJ = sum(SPLIT_SIZES)

kernel_name = "hybrid_diffattn_dsa_streaming_block"


def rmsnorm(x, w):
    xf = x.astype(jnp.float32)
    y = xf * lax.rsqrt(jnp.mean(xf * xf, axis=-1, keepdims=True) + EPS)
    return (y * w.astype(jnp.float32)).astype(x.dtype)


def t5_bucket(rel):
    nb = N_BUCKETS // 2
    ret = jnp.where(rel > 0, nb, 0)
    n = jnp.abs(rel)
    max_exact = nb // 2
    nf = jnp.maximum(n, 1).astype(jnp.float32)
    large = max_exact + (jnp.log(nf / max_exact) / math.log(MAX_DISTANCE / max_exact)
                         * (nb - max_exact)).astype(jnp.int32)
    large = jnp.minimum(large, nb - 1)
    return ret + jnp.where(n < max_exact, n, large)


def diff_attention(q1, q2, k1, k2, v, lam, bias_table):
    B, S, H, dv = v.shape
    pos = jnp.arange(S, dtype=jnp.int32)
    scale = A_QK_DIM ** -0.5

    def block(i):
        start = i * Q_BLOCK
        qb1 = lax.dynamic_slice_in_dim(q1, start, Q_BLOCK, axis=1)
        qb2 = lax.dynamic_slice_in_dim(q2, start, Q_BLOCK, axis=1)
        qpos = start + jnp.arange(Q_BLOCK, dtype=jnp.int32)
        rel = pos[None, :] - qpos[:, None]
        bias = jnp.transpose(bias_table[t5_bucket(rel)], (2, 0, 1)).astype(jnp.float32)
        allowed = (pos[None, :] // CHUNK) <= (qpos[:, None] // CHUNK)

        def probs(qb, k):
            s = jnp.einsum('bqhd,bkhd->bhqk', qb, k).astype(jnp.float32) * scale + bias
            s = jnp.where(allowed, s, -jnp.inf)
            return jax.nn.softmax(s, axis=-1)

        p = probs(qb1, k1) - lam * probs(qb2, k2)
        return jnp.einsum('bhqk,bkhd->bqhd', p.astype(v.dtype), v)

    out = lax.map(block, jnp.arange(S // Q_BLOCK))
    return jnp.transpose(out, (1, 0, 2, 3, 4)).reshape(B, S, H, dv)


def dsa_attention(q, k, v, iq, ik, iw, bias_table, k_sel):
    B, S, H, D = v.shape
    pos = jnp.arange(S, dtype=jnp.int32)
    scale = D ** -0.5
    idx_scale = (IDX_HEADS ** -0.5) * (IDX_DIM ** -0.5)
    gather = jax.vmap(lambda kk, ss: kk[ss])

    def block(i):
        start = i * SPARSE_Q_BLOCK
        qb = lax.dynamic_slice_in_dim(q, start, SPARSE_Q_BLOCK, axis=1)
        iqb = lax.dynamic_slice_in_dim(iq, start, SPARSE_Q_BLOCK, axis=1)
        iwb = lax.dynamic_slice_in_dim(iw, start, SPARSE_Q_BLOCK, axis=1)
        qpos = start + jnp.arange(SPARSE_Q_BLOCK, dtype=jnp.int32)
        allowed = (pos[None, :] // CHUNK) <= (qpos[:, None] // CHUNK)
        idx_logits = jnp.einsum('bqhd,bkd->bqhk', iqb, ik).astype(jnp.float32)
        score = jnp.einsum('bqh,bqhk->bqk', iwb.astype(jnp.float32) * idx_scale,
                           jax.nn.relu(idx_logits))
        score = jnp.where(allowed[None], score, -jnp.inf)
        _, sel = lax.top_k(score, k_sel)
        valid = (sel // CHUNK) <= (qpos[None, :, None] // CHUNK)
        kg = gather(k, sel)
        vg = gather(v, sel)
        s = jnp.einsum('bqhd,bqkhd->bhqk', qb, kg).astype(jnp.float32) * scale
        rel = sel - qpos[None, :, None]
        bias = jnp.transpose(bias_table[t5_bucket(rel)], (0, 3, 1, 2)).astype(jnp.float32)
        s = jnp.where(valid[:, None], s + bias, -jnp.inf)
        p = jax.nn.softmax(s, axis=-1)
        return jnp.einsum('bhqk,bqkhd->bqhd', p.astype(v.dtype), vg)

    out = lax.map(block, jnp.arange(S // SPARSE_Q_BLOCK))
    return jnp.transpose(out, (1, 0, 2, 3, 4)).reshape(B, S, H, D)


def hybrid_layer(x, norm_w, w_in, w_out, lq1, lk1, lq2, lk2, subln_w, rel_bias, layer_idx):
    B, S, _ = x.shape
    h = rmsnorm(x, norm_w)
    proj = jnp.einsum('bsd,de->bse', h, w_in)
    split_at = np.cumsum(SPLIT_SIZES)[:-1].tolist()
    aq, ak, av, ag, bq, bk, bv, bg, iq, ik, iw = jnp.split(proj, split_at, axis=-1)

    aq = aq.reshape(B, S, A_HEADS, 2, A_QK_DIM)
    ak = ak.reshape(B, S, A_HEADS, 2, A_QK_DIM)
    av = av.reshape(B, S, A_HEADS, A_V_DIM)
    lambda_init = 0.8 - 0.6 * math.exp(-0.3 * layer_idx)
    lam = (jnp.exp(jnp.sum(lq1.astype(jnp.float32) * lk1.astype(jnp.float32)))
           - jnp.exp(jnp.sum(lq2.astype(jnp.float32) * lk2.astype(jnp.float32)))
           + lambda_init)
    ao = diff_attention(aq[..., 0, :], aq[..., 1, :], ak[..., 0, :], ak[..., 1, :], av,
                        lam, rel_bias[:, :A_HEADS])
    ao = rmsnorm(ao, subln_w) * (1.0 - lambda_init)
    ao = ao.reshape(B, S, A_WIDTH) * jax.nn.silu(ag)

    k_sel = min(TOPK_MAX, S // 4)
    bo = dsa_attention(bq.reshape(B, S, B_HEADS, B_HEAD_DIM),
                       bk.reshape(B, S, B_HEADS, B_HEAD_DIM),
                       bv.reshape(B, S, B_HEADS, B_HEAD_DIM),
                       iq.reshape(B, S, IDX_HEADS, IDX_DIM), ik, iw,
                       rel_bias[:, A_HEADS:], k_sel)
    bo = bo.reshape(B, S, B_WIDTH) * jax.nn.silu(bg)

    y = jnp.einsum('bse,ed->bsd', jnp.concatenate([ao, bo], axis=-1), w_out)
    return x + y


def setup_inputs(seed: int = 0) -> dict:
    key = jax.random.key(seed)
    ks = jax.random.split(key, 12)
    f32 = jnp.float32
    x = jax.random.normal(ks[0], (BATCH, SEQ, D_MODEL), f32)
    norm_w = 1.0 + 0.01 * jax.random.normal(ks[1], (DEPTH, D_MODEL), f32)
    w_in = jax.random.normal(ks[2], (DEPTH, D_MODEL, D_IN_PROJ), f32) * D_MODEL ** -0.5
    w_out = jax.random.normal(ks[3], (DEPTH, D_MIX, D_MODEL), f32) * D_MIX ** -0.5
    lambda_q1 = 0.1 * jax.random.normal(ks[4], (DEPTH, A_QK_DIM), f32)
    lambda_k1 = 0.1 * jax.random.normal(ks[5], (DEPTH, A_QK_DIM), f32)
    lambda_q2 = 0.1 * jax.random.normal(ks[6], (DEPTH, A_QK_DIM), f32)
    lambda_k2 = 0.1 * jax.random.normal(ks[7], (DEPTH, A_QK_DIM), f32)
    subln_w = 1.0 + 0.01 * jax.random.normal(ks[8], (DEPTH, A_V_DIM), f32)
    rel_bias = 0.5 * jax.random.normal(ks[9], (N_BUCKETS, N_BIAS_HEADS), f32)
    final_norm_w = 1.0 + 0.01 * jax.random.normal(ks[10], (D_MODEL,), f32)
    return {"x": x, "norm_w": norm_w, "w_in": w_in, "w_out": w_out,
            "lambda_q1": lambda_q1, "lambda_k1": lambda_k1,
            "lambda_q2": lambda_q2, "lambda_k2": lambda_k2,
            "subln_w": subln_w, "rel_bias": rel_bias, "final_norm_w": final_norm_w}


def reference(x, norm_w, w_in, w_out, lambda_q1, lambda_k1, lambda_q2, lambda_k2,
              subln_w, rel_bias, final_norm_w):
    for l in range(DEPTH):
        x = hybrid_layer(x, norm_w[l], w_in[l], w_out[l], lambda_q1[l], lambda_k1[l],
                         lambda_q2[l], lambda_k2[l], subln_w[l], rel_bias, l)
    return rmsnorm(x, final_norm_w)
```

```python
import functools
import math

import jax
import jax.numpy as jnp
from jax import lax
from jax.experimental import pallas as pl
from jax.experimental.pallas import tpu as pltpu

F32 = jnp.float32
BF16 = jnp.bfloat16
I32 = jnp.int32

EPS = 1e-6
CHUNK = 64
A_HEADS = 4
A_QK_DIM = 64
A_V_DIM = 128
B_HEADS = 8
B_HEAD_DIM = 64
IDX_HEADS = 4
IDX_DIM = 64
TOPK_MAX = 256
N_BUCKETS = 32
HEAD_W = 512
LANES = 128
ROWS = 128
KV_TILE = 256
PROJ_TM = 512
INT_MIN = -2 ** 31
NEG_INF = float("-inf")
VMEM_LIMIT = 52 * 1024 * 1024

_T5_THRESHOLDS = (12, 16, 23, 32, 46, 64, 91)


def _dot_nt(a, b):
    return lax.dot_general(a, b, (((1,), (1,)), ((), ())), preferred_element_type=F32)


def _inproj_kernel(x_ref, nw_ref, w_ref, aq, ak, av, ag, bq, bk, bv, bg, iq, ik, iw,
                   *, q_scale, idx_scale):
    x = x_ref[...]
    ms = jnp.mean(x * x, axis=-1, keepdims=True)
    h = ((x * lax.rsqrt(ms + EPS)) * nw_ref[...]).astype(BF16)

    def proj(c):
        return jnp.dot(h, w_ref[:, c * HEAD_W:(c + 1) * HEAD_W], preferred_element_type=F32)

    aq[...] = (proj(0) * q_scale).astype(BF16)
    ak[...] = proj(1).astype(BF16)
    av[...] = proj(2).astype(BF16)
    ag[...] = proj(3)
    bq[...] = (proj(4) * q_scale).astype(BF16)
    bk[...] = proj(5).astype(BF16)
    bv[...] = proj(6).astype(BF16)
    bg[...] = proj(7)
    tail = proj(8)
    iq[...] = tail[:, 0:256].astype(BF16)
    ik[...] = tail[:, 256:384].astype(BF16)
    iw[...] = tail[:, 384:512] * idx_scale


def _inproj(x2, norm_w, w_pad):
    n, d = x2.shape
    grid = (n // PROJ_TM,)
    row = lambda w: pl.BlockSpec((PROJ_TM, w), lambda i: (i, 0))
    out_shape = ([jax.ShapeDtypeStruct((n, HEAD_W), BF16)] * 3 + [jax.ShapeDtypeStruct((n, HEAD_W), F32)]) * 2 + [
        jax.ShapeDtypeStruct((n, 256), BF16), jax.ShapeDtypeStruct((n, LANES), BF16),
        jax.ShapeDtypeStruct((n, LANES), F32)]
    out_specs = [row(HEAD_W)] * 8 + [row(256), row(LANES), row(LANES)]
    kern = functools.partial(_inproj_kernel, q_scale=A_QK_DIM ** -0.5,
                             idx_scale=(IDX_HEADS ** -0.5) * (IDX_DIM ** -0.5))
    return pl.pallas_call(
        kern, grid=grid,
        in_specs=[row(d), pl.BlockSpec((1, d), lambda i: (0, 0)),
                  pl.BlockSpec(w_pad.shape, lambda i: (0, 0))],
        out_specs=out_specs, out_shape=out_shape,
        compiler_params=pltpu.CompilerParams(dimension_semantics=("arbitrary",),
                                             vmem_limit_bytes=VMEM_LIMIT),
        name="inproj",
    )(x2, norm_w.reshape(1, d), w_pad)


def _bias_kernel(tab_ref, out_ref):
    h = pl.program_id(0)
    t = lax.broadcasted_iota(I32, (KV_TILE, KV_TILE), 0)
    s = lax.broadcasted_iota(I32, (KV_TILE, KV_TILE), 1)
    far = tab_ref[N_BUCKETS // 2 - 1, h]
    for kind in (0, 1):
        rel = s - t - KV_TILE * kind
        n = jnp.abs(rel)
        large = jnp.full_like(n, N_BUCKETS // 4)
        for thr in _T5_THRESHOLDS:
            large = large + (n >= thr).astype(I32)
        bucket = jnp.where(rel > 0, N_BUCKETS // 2, 0) + jnp.where(n < N_BUCKETS // 4, n, large)
        bias = jnp.zeros((KV_TILE, KV_TILE), F32)
        for b in range(N_BUCKETS):
            bias = jnp.where(bucket == b, tab_ref[b, h], bias)
        bias = bias - far
        if kind == 0:
            blocked = jnp.logical_and(h < A_HEADS, (s // CHUNK) > (t // CHUNK))
            bias = jnp.where(blocked, NEG_INF, bias)
        out_ref[kind] = bias
    out_ref[2] = jnp.zeros((KV_TILE, KV_TILE), F32)


def _bias_tiles(rel_bias):
    nh = rel_bias.shape[1]
    return pl.pallas_call(
        _bias_kernel, grid=(nh,),
        in_specs=[pl.BlockSpec(memory_space=pltpu.SMEM)],
        out_specs=pl.BlockSpec((None, 3, KV_TILE, KV_TILE), lambda h: (h, 0, 0, 0)),
        out_shape=jax.ShapeDtypeStruct((nh, 3, KV_TILE, KV_TILE), F32),
        name="bias_tiles",
    )(rel_bias)


def _attend(qm, k_ref, v_ref, col0, n_tiles, extra, s_scr):
    def p1(j, mrun):
        start = pl.multiple_of(j * KV_TILE, KV_TILE)
        kc = k_ref[pl.ds(start, KV_TILE), col0:col0 + LANES]
        s = _dot_nt(qm, kc) + extra(j, start)
        s_scr[:, pl.ds(start, KV_TILE)] = s
        return jnp.maximum(mrun, jnp.maximum(s[:, :LANES], s[:, LANES:]))

    mrun = lax.fori_loop(0, n_tiles, p1, jnp.full((ROWS, LANES), NEG_INF, F32))
    m = jnp.max(mrun, axis=1, keepdims=True)

    def p2(j, carry):
        lrun, acc = carry
        start = pl.multiple_of(j * KV_TILE, KV_TILE)
        p = jnp.exp(s_scr[:, pl.ds(start, KV_TILE)] - m)
        lrun = lrun + p[:, :LANES] + p[:, LANES:]
        vc = v_ref[pl.ds(start, KV_TILE), col0:col0 + LANES]
        acc = acc + jnp.dot(p.astype(BF16), vc, preferred_element_type=F32)
        return lrun, acc

    zero = jnp.zeros((ROWS, LANES), F32)
    lrun, acc = lax.fori_loop(0, n_tiles, p2, (zero, zero))
    return acc / jnp.sum(lrun, axis=1, keepdims=True)


def _silu(g):
    return g * (1.0 / (1.0 + jnp.exp(-g)))


def _diff_kernel(lq1, lk1, lq2, lk2, subw_ref, q_ref, k_ref, v_ref, g_ref, bias_ref, o_ref,
                 s_scr, *, lambda_init):
    tb = pl.program_id(2)
    i = tb // 2
    r0 = pl.multiple_of((tb % 2) * ROWS, ROWS)
    lam = (jnp.exp(jnp.sum(lq1[...] * lk1[...], axis=-1, keepdims=True))
           - jnp.exp(jnp.sum(lq2[...] * lk2[...], axis=-1, keepdims=True)) + lambda_init)
    lane = lax.broadcasted_iota(I32, (1, LANES), 1)
    q = q_ref[...]

    def extra(j, start):
        return bias_ref[jnp.minimum(i - j, 2), pl.ds(r0, ROWS), :]

    outs = []
    for first in (True, False):
        qm = jnp.where((lane < A_QK_DIM) == first, q, jnp.zeros_like(q))
        outs.append(_attend(qm, k_ref, v_ref, 0, i + 1, extra, s_scr))
    o = outs[0] - lam * outs[1]
    y = (o * lax.rsqrt(jnp.mean(o * o, axis=-1, keepdims=True) + EPS)) * subw_ref[...]
    y = y * (1.0 - lambda_init)
    o_ref[...] = (y * _silu(g_ref[...])).astype(BF16)


def _diff_attention(aq, ak, av, ag, bias, lq1, lk1, lq2, lk2, subln_w, lambda_init):
    b, s, _ = aq.shape
    grid = (b, A_HEADS, s // ROWS)
    vec = lambda a: pl.BlockSpec((1, a.shape[-1]), lambda bi, h, t: (0, 0))
    rows = pl.BlockSpec((None, ROWS, LANES), lambda bi, h, t: (bi, t, h))
    full = pl.BlockSpec((None, s, LANES), lambda bi, h, t: (bi, 0, h))
    kern = functools.partial(_diff_kernel, lambda_init=lambda_init)
    return pl.pallas_call(
        kern, grid=grid,
        in_specs=[vec(lq1), vec(lk1), vec(lq2), vec(lk2), vec(subln_w), rows, full, full, rows,
                  pl.BlockSpec((None, 3, KV_TILE, KV_TILE), lambda bi, h, t: (h, 0, 0, 0))],
        out_specs=rows,
        out_shape=jax.ShapeDtypeStruct((b, s, A_HEADS * A_V_DIM), BF16),
        scratch_shapes=[pltpu.VMEM((ROWS, s), F32)],
        compiler_params=pltpu.CompilerParams(dimension_semantics=("arbitrary",) * 3,
                                             vmem_limit_bytes=VMEM_LIMIT),
        name="diff_attention",
    )(lq1, lk1, lq2, lk2, subln_w, aq, ak, av, ag, bias)


def _dsa_kernel(iq_ref, ik_ref, iw_ref, q_ref, k_ref, v_ref, g_ref, bias_ref, o_ref,
                key_scr, m_scr, s_scr, *, k_sel):
    tb = pl.program_id(1)
    i = tb // 2
    n_tiles = i + 1
    r0 = pl.multiple_of((tb % 2) * ROWS, ROWS)
    lane = lax.broadcasted_iota(I32, (1, LANES), 1)
    low = lane < B_HEAD_DIM
    row_chunk = (tb * ROWS + lax.broadcasted_iota(I32, (ROWS, KV_TILE), 0)) // CHUNK
    col_in_tile = lax.broadcasted_iota(I32, (ROWS, KV_TILE), 1)

    iq = iq_ref[...]
    iw = iw_ref[...]
    iq_heads = []
    for hh in range(IDX_HEADS):
        pair = iq[:, (hh // 2) * LANES:(hh // 2 + 1) * LANES]
        iq_heads.append(jnp.where(low == (hh % 2 == 0), pair, jnp.zeros_like(pair)))

    def score_tile(j, carry):
        start = pl.multiple_of(j * KV_TILE, KV_TILE)
        ikc = ik_ref[pl.ds(start, KV_TILE), :]
        sc = jnp.zeros((ROWS, KV_TILE), F32)
        for hh in range(IDX_HEADS):
            sc = sc + iw[:, hh:hh + 1] * jnp.maximum(_dot_nt(iq_heads[hh], ikc), 0.0)
        allowed = ((start + col_in_tile) // CHUNK) <= row_chunk
        sc = jnp.where(allowed, sc, NEG_INF)
        bits = lax.bitcast_convert_type(sc, I32)
        key = bits ^ (lax.shift_right_arithmetic(bits, 31) & 0x7FFFFFFF)
        key_scr[:, pl.ds(start, KV_TILE)] = key
        return carry

    lax.fori_loop(0, n_tiles, score_tile, 0)

    def count(pred):
        def body(j, acc):
            start = pl.multiple_of(j * KV_TILE, KV_TILE)
            hit = jnp.where(pred(key_scr[:, pl.ds(start, KV_TILE)], start), 1.0, 0.0)
            return acc + hit[:, :LANES] + hit[:, LANES:]
        acc = lax.fori_loop(0, n_tiles, body, jnp.zeros((ROWS, LANES), F32))
        return jnp.sum(acc, axis=1, keepdims=True)

    kf = float(k_sel)

    def bit_step(b, carry):
        thr, c_thr = carry
        cand = thr + lax.shift_left(jnp.int32(1), 31 - b)
        c = count(lambda kc, start: kc >= cand)
        ok = c >= kf
        return jnp.where(ok, cand, thr), jnp.where(ok, c, c_thr)

    thr0 = jnp.full((ROWS, 1), INT_MIN, I32)
    c0 = jnp.full((ROWS, 1), 1.0, F32) * (n_tiles * KV_TILE).astype(F32)
    thr, c_thr = lax.fori_loop(0, 32, bit_step, (thr0, c0))

    def tie_limit():
        need = kf - count(lambda kc, start: kc > thr)

        def idx_step(b, lim):
            cand = lim + lax.shift_left(jnp.int32(1), 11 - b)
            f = count(lambda kc, start: jnp.logical_and(kc == thr, (start + col_in_tile) < cand))
            return jnp.where(f < need, cand, lim)

        return lax.fori_loop(0, 12, idx_step, jnp.zeros((ROWS, 1), I32))

    lim = lax.cond(jnp.max(c_thr) > kf, tie_limit,
                   lambda: jnp.full((ROWS, 1), 2 ** 30, I32))

    def mask_tile(j, carry):
        start = pl.multiple_of(j * KV_TILE, KV_TILE)
        kc = key_scr[:, pl.ds(start, KV_TILE)]
        col = start + col_in_tile
        sel = jnp.logical_or(kc > thr, jnp.logical_and(kc == thr, col <= lim))
        sel = jnp.logical_and(sel, (col // CHUNK) <= row_chunk)
        m_scr[:, pl.ds(start, KV_TILE)] = jnp.where(sel, 0.0, NEG_INF)
        return carry

    lax.fori_loop(0, n_tiles, mask_tile, 0)

    q = q_ref[...]
    g = g_ref[...]
    for pair in range(B_HEADS // 2):
        col0 = pair * LANES
        qp = q[:, col0:col0 + LANES]
        outs = []
        for first in (True, False):
            head = 2 * pair + (0 if first else 1)
            qm = jnp.where(low == first, qp, jnp.zeros_like(qp))

            def extra(j, start, head=head):
                return (m_scr[:, pl.ds(start, KV_TILE)]
                        + bias_ref[head, jnp.minimum(i - j, 2), pl.ds(r0, ROWS), :])

            outs.append(_attend(qm, k_ref, v_ref, col0, n_tiles, extra, s_scr))
        o = jnp.where(low, outs[0], outs[1])
        o_ref[:, col0:col0 + LANES] = (o * _silu(g[:, col0:col0 + LANES])).astype(BF16)


def _dsa_attention(iq, ik, iw, bq, bk, bv, bg, bias, k_sel):
    b, s, _ = bq.shape
    grid = (b, s // ROWS)
    rows = lambda w: pl.BlockSpec((None, ROWS, w), lambda bi, t: (bi, t, 0))
    full = lambda w: pl.BlockSpec((None, s, w), lambda bi, t: (bi, 0, 0))
    kern = functools.partial(_dsa_kernel, k_sel=k_sel)
    return pl.pallas_call(
        kern, grid=grid,
        in_specs=[rows(256), full(LANES), rows(LANES), rows(HEAD_W), full(HEAD_W), full(HEAD_W),
                  rows(HEAD_W),
                  pl.BlockSpec((B_HEADS, 3, KV_TILE, KV_TILE), lambda bi, t: (0, 0, 0, 0))],
        out_specs=rows(HEAD_W),
        out_shape=jax.ShapeDtypeStruct((b, s, HEAD_W), BF16),
        scratch_shapes=[pltpu.VMEM((ROWS, s), I32), pltpu.VMEM((ROWS, s), F32),
                        pltpu.VMEM((ROWS, s), F32)],
        compiler_params=pltpu.CompilerParams(dimension_semantics=("arbitrary",) * 2,
                                             vmem_limit_bytes=VMEM_LIMIT),
        name="dsa_attention",
    )(iq, ik, iw, bq, bk, bv, bg, bias)


def _outproj_kernel(ao_ref, bo_ref, x_ref, w_ref, fw_ref, o_ref):
    y = jnp.dot(ao_ref[...], w_ref[0:HEAD_W, :], preferred_element_type=F32)
    y = y + jnp.dot(bo_ref[...], w_ref[HEAD_W:2 * HEAD_W, :], preferred_element_type=F32)
    z = x_ref[...] + y
    o_ref[...] = (z * lax.rsqrt(jnp.mean(z * z, axis=-1, keepdims=True) + EPS)) * fw_ref[...]


def _outproj(ao, bo, x2, w_out, final_w):
    n, d = x2.shape
    row = lambda w: pl.BlockSpec((PROJ_TM, w), lambda i: (i, 0))
    return pl.pallas_call(
        _outproj_kernel, grid=(n // PROJ_TM,),
        in_specs=[row(HEAD_W), row(HEAD_W), row(d), pl.BlockSpec(w_out.shape, lambda i: (0, 0)),
                  pl.BlockSpec((1, d), lambda i: (0, 0))],
        out_specs=row(d), out_shape=jax.ShapeDtypeStruct((n, d), F32),
        compiler_params=pltpu.CompilerParams(dimension_semantics=("arbitrary",),
                                             vmem_limit_bytes=VMEM_LIMIT),
        name="outproj",
    )(ao, bo, x2, w_out, final_w.reshape(1, d))


def kernel(x, norm_w, w_in, w_out, lambda_q1, lambda_k1, lambda_q2, lambda_k2, subln_w, rel_bias,
           final_norm_w):
    b, s, d = x.shape
    depth = norm_w.shape[0]
    assert depth == 1, "single-layer trunk"
    assert s % KV_TILE == 0 and (b * s) % PROJ_TM == 0
    layer = 0
    lambda_init = 0.8 - 0.6 * math.exp(-0.3 * layer)
    k_sel = min(TOPK_MAX, s // 4)

    w = w_in[layer]
    main = 8 * HEAD_W
    iq_w = w[:, main:main + IDX_HEADS * IDX_DIM]
    ik_w = w[:, main + IDX_HEADS * IDX_DIM:main + IDX_HEADS * IDX_DIM + IDX_DIM]
    iw_w = w[:, main + IDX_HEADS * IDX_DIM + IDX_DIM:]
    pad = jnp.zeros((d, LANES - IDX_HEADS), w.dtype)
    w_pad = jnp.concatenate([w[:, :main], iq_w, ik_w, ik_w, iw_w, pad], axis=1).astype(BF16)

    x2 = x.reshape(b * s, d)
    aq, ak, av, ag, bq, bk, bv, bg, iq, ik, iw = _inproj(x2, norm_w[layer], w_pad)
    r3 = lambda a: a.reshape(b, s, a.shape[-1])
    bias = _bias_tiles(rel_bias)

    vec = lambda a: a.reshape(1, -1)
    ao = _diff_attention(r3(aq), r3(ak), r3(av), r3(ag), bias,
                         vec(lambda_q1[layer]), vec(lambda_k1[layer]),
                         vec(lambda_q2[layer]), vec(lambda_k2[layer]), vec(subln_w[layer]),
                         lambda_init)
    bo = _dsa_attention(r3(iq), r3(ik), r3(iw), r3(bq), r3(bk), r3(bv), r3(bg),
                        bias[A_HEADS:], k_sel)

    out = _outproj(ao.reshape(b * s, -1), bo.reshape(b * s, -1), x2,
                   w_out[layer].astype(BF16), final_norm_w)
    return out.reshape(b, s, d)
```

```python
import functools
import math

import jax
import jax.numpy as jnp
from jax import lax
from jax.experimental import pallas as pl
from jax.experimental.pallas import tpu as pltpu

F32 = jnp.float32
BF16 = jnp.bfloat16
I32 = jnp.int32

EPS = 1e-6
CHUNK = 64
A_HEADS = 4
A_QK_DIM = 64
A_V_DIM = 128
B_HEADS = 8
B_HEAD_DIM = 64
IDX_HEADS = 4
IDX_DIM = 64
TOPK_MAX = 256
N_BUCKETS = 32
HEAD_W = 512
LANES = 128
ROWS = 128
KV_TILE = 256
N_UNITS = 8
PROJ_TM = 512
INT_MIN = -2 ** 31
NEG_INF = float("-inf")
VMEM_LIMIT = 56 * 1024 * 1024

_T5_THRESHOLDS = (12, 16, 23, 32, 46, 64, 91)


def _dot_nt(a, b):
    return lax.dot_general(a, b, (((1,), (1,)), ((), ())), preferred_element_type=F32)


def _resident(shape, index_map):
    return pl.BlockSpec(shape, index_map, pipeline_mode=pl.Buffered(1))


def _inproj_kernel(x_ref, nw_ref, w_ref, aq, ak, av, ag, bq, bk, bv, bg, iq, ik, iw,
                   *, q_scale, idx_scale):
    x = x_ref[...]
    ms = jnp.mean(x * x, axis=-1, keepdims=True)
    h = ((x * lax.rsqrt(ms + EPS)) * nw_ref[...]).astype(BF16)

    def proj(c):
        return jnp.dot(h, w_ref[:, c * HEAD_W:(c + 1) * HEAD_W], preferred_element_type=F32)

    aq[...] = (proj(0) * q_scale).astype(BF16)
    ak[...] = proj(1).astype(BF16)
    av[...] = proj(2).astype(BF16)
    ag[...] = proj(3)
    bq[...] = (proj(4) * q_scale).astype(BF16)
    bk[...] = proj(5).astype(BF16)
    bv[...] = proj(6).astype(BF16)
    bg[...] = proj(7)
    tail = proj(8)
    iq[...] = tail[:, 0:256].astype(BF16)
    ik[...] = tail[:, 256:384].astype(BF16)
    iw[...] = tail[:, 384:512] * idx_scale


def _inproj(x2, norm_w, w_pad):
    n, d = x2.shape
    grid = (n // PROJ_TM,)
    row = lambda w: pl.BlockSpec((PROJ_TM, w), lambda i: (i, 0))
    out_shape = ([jax.ShapeDtypeStruct((n, HEAD_W), BF16)] * 3 + [jax.ShapeDtypeStruct((n, HEAD_W), F32)]) * 2 + [
        jax.ShapeDtypeStruct((n, 256), BF16), jax.ShapeDtypeStruct((n, LANES), BF16),
        jax.ShapeDtypeStruct((n, LANES), F32)]
    out_specs = [row(HEAD_W)] * 8 + [row(256), row(LANES), row(LANES)]
    kern = functools.partial(_inproj_kernel, q_scale=A_QK_DIM ** -0.5,
                             idx_scale=(IDX_HEADS ** -0.5) * (IDX_DIM ** -0.5))
    return pl.pallas_call(
        kern, grid=grid,
        in_specs=[row(d), pl.BlockSpec((1, d), lambda i: (0, 0)),
                  pl.BlockSpec(w_pad.shape, lambda i: (0, 0))],
        out_specs=out_specs, out_shape=out_shape,
        compiler_params=pltpu.CompilerParams(dimension_semantics=("arbitrary",),
                                             vmem_limit_bytes=VMEM_LIMIT),
        name="inproj",
    )(x2, norm_w.reshape(1, d), w_pad)


def _bias_kernel(tab_ref, out_ref):
    h = pl.program_id(0)
    t = lax.broadcasted_iota(I32, (KV_TILE, KV_TILE), 0)
    s = lax.broadcasted_iota(I32, (KV_TILE, KV_TILE), 1)
    far = tab_ref[N_BUCKETS // 2 - 1, h]
    for kind in (0, 1):
        rel = s - t - KV_TILE * kind
        n = jnp.abs(rel)
        large = jnp.full_like(n, N_BUCKETS // 4)
        for thr in _T5_THRESHOLDS:
            large = large + (n >= thr).astype(I32)
        bucket = jnp.where(rel > 0, N_BUCKETS // 2, 0) + jnp.where(n < N_BUCKETS // 4, n, large)
        bias = jnp.zeros((KV_TILE, KV_TILE), F32)
        for b in range(N_BUCKETS):
            bias = jnp.where(bucket == b, tab_ref[b, h], bias)
        bias = bias - far
        if kind == 0:
            blocked = jnp.logical_and(h < A_HEADS, (s // CHUNK) > (t // CHUNK))
            bias = jnp.where(blocked, NEG_INF, bias)
        out_ref[kind] = bias


def _bias_tiles(rel_bias):
    nh = rel_bias.shape[1]
    return pl.pallas_call(
        _bias_kernel, grid=(nh,),
        in_specs=[pl.BlockSpec(memory_space=pltpu.SMEM)],
        out_specs=pl.BlockSpec((None, 2, KV_TILE, KV_TILE), lambda h: (h, 0, 0, 0)),
        out_shape=jax.ShapeDtypeStruct((nh, 2, KV_TILE, KV_TILE), F32),
        name="bias_tiles",
    )(rel_bias)


def _stack_masked_queries(q, qm_scr):
    low = lax.broadcasted_iota(I32, (1, LANES), 1) < LANES // 2
    for p in range(N_UNITS // 2):
        qp = q[:, p * LANES:(p + 1) * LANES]
        zero = jnp.zeros_like(qp)
        qm_scr[p, 0:ROWS, :] = jnp.where(low, qp, zero)
        qm_scr[p, ROWS:2 * ROWS, :] = jnp.where(low, zero, qp)


def _two_pass_attention(i, r0, qm_scr, k_ref, v_ref, bias_ref, bias_head, mask_scr,
                        s_scr, mx_scr, l_scr, acc_scr):
    n_tiles = i + 1
    n_far = jnp.maximum(i - 1, 0)
    mx_scr[...] = jnp.full(mx_scr.shape, NEG_INF, F32)
    l_scr[...] = jnp.zeros(l_scr.shape, F32)
    acc_scr[...] = jnp.zeros(acc_scr.shape, F32)

    def logits_pass(near):
        def body(j, carry):
            start = pl.multiple_of(j * KV_TILE, KV_TILE)
            add = None if mask_scr is None else mask_scr[:, pl.ds(start, KV_TILE)]
            for p in range(N_UNITS // 2):
                kc = k_ref[pl.ds(start, KV_TILE), p * LANES:(p + 1) * LANES]
                s2 = _dot_nt(qm_scr[p], kc)
                for half in range(2):
                    u = 2 * p + half
                    s = s2[half * ROWS:(half + 1) * ROWS]
                    if add is not None:
                        s = s + add
                    if near:
                        s = s + bias_ref[bias_head(u), i - j, pl.ds(r0, ROWS), :]
                    s_scr[u, :, pl.ds(start, KV_TILE)] = s
                    mx_scr[u] = jnp.maximum(mx_scr[u], jnp.maximum(s[:, :LANES], s[:, LANES:]))
            return carry
        return body

    lax.fori_loop(0, n_far, logits_pass(False), 0)
    lax.fori_loop(n_far, n_tiles, logits_pass(True), 0)

    for u in range(N_UNITS):
        m = jnp.max(mx_scr[u], axis=1, keepdims=True)
        mx_scr[u] = jnp.broadcast_to(m, (ROWS, LANES))

    def exp_pass(j, carry):
        start = pl.multiple_of(j * KV_TILE, KV_TILE)
        for u in range(N_UNITS):
            m = mx_scr[u]
            s = s_scr[u, :, pl.ds(start, KV_TILE)]
            p_lo = jnp.exp(s[:, :LANES] - m)
            p_hi = jnp.exp(s[:, LANES:] - m)
            l_scr[u] = l_scr[u] + (p_lo + p_hi)
            pb = jnp.concatenate([p_lo, p_hi], axis=1).astype(BF16)
            g = u // 2
            vc = v_ref[pl.ds(start, KV_TILE), g * LANES:(g + 1) * LANES]
            acc_scr[u] = acc_scr[u] + jnp.dot(pb, vc, preferred_element_type=F32)
        return carry

    lax.fori_loop(0, n_tiles, exp_pass, 0)


def _normalised(u, l_scr, acc_scr):
    return acc_scr[u] / jnp.sum(l_scr[u], axis=1, keepdims=True)


def _silu(g):
    return g * (1.0 / (1.0 + jnp.exp(-g)))


def _attention_scratch(s):
    return [pltpu.VMEM((N_UNITS // 2, 2 * ROWS, LANES), BF16),
            pltpu.VMEM((N_UNITS, ROWS, s), F32),
            pltpu.VMEM((N_UNITS, ROWS, LANES), F32),
            pltpu.VMEM((N_UNITS, ROWS, LANES), F32),
            pltpu.VMEM((N_UNITS, ROWS, LANES), F32)]


def _diff_kernel(lq1, lk1, lq2, lk2, subw_ref, q_ref, k_ref, v_ref, g_ref, bias_ref, o_ref,
                 qm_scr, s_scr, mx_scr, l_scr, acc_scr, *, lambda_init):
    tb = pl.program_id(1)
    i = tb // 2
    r0 = pl.multiple_of((tb % 2) * ROWS, ROWS)
    lam = (jnp.exp(jnp.sum(lq1[...] * lk1[...], axis=-1, keepdims=True))
           - jnp.exp(jnp.sum(lq2[...] * lk2[...], axis=-1, keepdims=True)) + lambda_init)
    _stack_masked_queries(q_ref[...], qm_scr)
    _two_pass_attention(i, r0, qm_scr, k_ref, v_ref, bias_ref, lambda u: u // 2, None,
                        s_scr, mx_scr, l_scr, acc_scr)
    for h in range(A_HEADS):
        o = _normalised(2 * h, l_scr, acc_scr) - lam * _normalised(2 * h + 1, l_scr, acc_scr)
        y = (o * lax.rsqrt(jnp.mean(o * o, axis=-1, keepdims=True) + EPS)) * subw_ref[...]
        y = y * (1.0 - lambda_init)
        cols = slice(h * LANES, (h + 1) * LANES)
        o_ref[:, cols] = (y * _silu(g_ref[:, cols])).astype(BF16)


def _diff_attention(aq, ak, av, ag, bias, lq1, lk1, lq2, lk2, subln_w, lambda_init):
    b, s, _ = aq.shape
    grid = (b, s // ROWS)
    vec = lambda a: pl.BlockSpec((1, a.shape[-1]), lambda bi, t: (0, 0))
    rows = pl.BlockSpec((None, ROWS, HEAD_W), lambda bi, t: (bi, t, 0))
    full = _resident((None, s, HEAD_W), lambda bi, t: (bi, 0, 0))
    kern = functools.partial(_diff_kernel, lambda_init=lambda_init)
    return pl.pallas_call(
        kern, grid=grid,
        in_specs=[vec(lq1), vec(lk1), vec(lq2), vec(lk2), vec(subln_w), rows, full, full, rows,
                  _resident(bias.shape, lambda bi, t: (0, 0, 0, 0))],
        out_specs=rows,
        out_shape=jax.ShapeDtypeStruct((b, s, HEAD_W), BF16),
        scratch_shapes=_attention_scratch(s),
        compiler_params=pltpu.CompilerParams(dimension_semantics=("arbitrary",) * 2,
                                             vmem_limit_bytes=VMEM_LIMIT),
        name="diff_attention",
    )(lq1, lk1, lq2, lk2, subln_w, aq, ak, av, ag, bias)


def _dsa_kernel(iq_ref, ik_ref, iw_ref, q_ref, k_ref, v_ref, g_ref, bias_ref, o_ref,
                key_scr, m_scr, qm_scr, s_scr, mx_scr, l_scr, acc_scr, *, k_sel):
    tb = pl.program_id(1)
    i = tb // 2
    n_tiles = i + 1
    r0 = pl.multiple_of((tb % 2) * ROWS, ROWS)
    lane = lax.broadcasted_iota(I32, (1, LANES), 1)
    low = lane < B_HEAD_DIM
    row_chunk = (tb * ROWS + lax.broadcasted_iota(I32, (ROWS, KV_TILE), 0)) // CHUNK
    col_in_tile = lax.broadcasted_iota(I32, (ROWS, KV_TILE), 1)

    iq = iq_ref[...]
    iw = iw_ref[...]
    iq_heads = []
    for hh in range(IDX_HEADS):
        pair = iq[:, (hh // 2) * LANES:(hh // 2 + 1) * LANES]
        iq_heads.append(jnp.where(low == (hh % 2 == 0), pair, jnp.zeros_like(pair)))

    def score_tile(j, carry):
        start = pl.multiple_of(j * KV_TILE, KV_TILE)
        ikc = ik_ref[pl.ds(start, KV_TILE), :]
        sc = jnp.zeros((ROWS, KV_TILE), F32)
        for hh in range(IDX_HEADS):
            sc = sc + iw[:, hh:hh + 1] * jnp.maximum(_dot_nt(iq_heads[hh], ikc), 0.0)
        allowed = ((start + col_in_tile) // CHUNK) <= row_chunk
        sc = jnp.where(allowed, sc, NEG_INF)
        bits = lax.bitcast_convert_type(sc, I32)
        key = bits ^ (lax.shift_right_arithmetic(bits, 31) & 0x7FFFFFFF)
        key_scr[:, pl.ds(start, KV_TILE)] = key
        return carry

    lax.fori_loop(0, n_tiles, score_tile, 0)

    def count(pred):
        def body(j, acc):
            start = pl.multiple_of(j * KV_TILE, KV_TILE)
            hit = jnp.where(pred(key_scr[:, pl.ds(start, KV_TILE)], start), 1.0, 0.0)
            return acc + hit[:, :LANES] + hit[:, LANES:]
        acc = lax.fori_loop(0, n_tiles, body, jnp.zeros((ROWS, LANES), F32))
        return jnp.sum(acc, axis=1, keepdims=True)

    kf = float(k_sel)

    def bit_step(b, carry):
        thr, c_thr = carry
        cand = thr + lax.shift_left(jnp.int32(1), 31 - b)
        c = count(lambda kc, start: kc >= cand)
        ok = c >= kf
        return jnp.where(ok, cand, thr), jnp.where(ok, c, c_thr)

    thr0 = jnp.full((ROWS, 1), INT_MIN, I32)
    c0 = jnp.full((ROWS, 1), 1.0, F32) * (n_tiles * KV_TILE).astype(F32)
    thr, c_thr = lax.fori_loop(0, 32, bit_step, (thr0, c0))

    def tie_limit():
        need = kf - count(lambda kc, start: kc > thr)

        def idx_step(b, lim):
            cand = lim + lax.shift_left(jnp.int32(1), 11 - b)
            f = count(lambda kc, start: jnp.logical_and(kc == thr, (start + col_in_tile) < cand))
            return jnp.where(f < need, cand, lim)

        return lax.fori_loop(0, 12, idx_step, jnp.zeros((ROWS, 1), I32))

    lim = lax.cond(jnp.max(c_thr) > kf, tie_limit,
                   lambda: jnp.full((ROWS, 1), 2 ** 30, I32))

    def mask_tile(j, carry):
        start = pl.multiple_of(j * KV_TILE, KV_TILE)
        kc = key_scr[:, pl.ds(start, KV_TILE)]
        col = start + col_in_tile
        sel = jnp.logical_or(kc > thr, jnp.logical_and(kc == thr, col <= lim))
        sel = jnp.logical_and(sel, (col // CHUNK) <= row_chunk)
        m_scr[:, pl.ds(start, KV_TILE)] = jnp.where(sel, 0.0, NEG_INF)
        return carry

    lax.fori_loop(0, n_tiles, mask_tile, 0)

    _stack_masked_queries(q_ref[...], qm_scr)
    _two_pass_attention(i, r0, qm_scr, k_ref, v_ref, bias_ref, lambda u: u, m_scr,
                        s_scr, mx_scr, l_scr, acc_scr)
    for p in range(B_HEADS // 2):
        o = jnp.where(low, _normalised(2 * p, l_scr, acc_scr), _normalised(2 * p + 1, l_scr, acc_scr))
        cols = slice(p * LANES, (p + 1) * LANES)
        o_ref[:, cols] = (o * _silu(g_ref[:, cols])).astype(BF16)


def _dsa_attention(iq, ik, iw, bq, bk, bv, bg, bias, k_sel):
    b, s, _ = bq.shape
    grid = (b, s // ROWS)
    rows = lambda w: pl.BlockSpec((None, ROWS, w), lambda bi, t: (bi, t, 0))
    full = lambda w: _resident((None, s, w), lambda bi, t: (bi, 0, 0))
    kern = functools.partial(_dsa_kernel, k_sel=k_sel)
    return pl.pallas_call(
        kern, grid=grid,
        in_specs=[rows(256), full(LANES), rows(LANES), rows(HEAD_W), full(HEAD_W), full(HEAD_W),
                  rows(HEAD_W), _resident(bias.shape, lambda bi, t: (0, 0, 0, 0))],
        out_specs=rows(HEAD_W),
        out_shape=jax.ShapeDtypeStruct((b, s, HEAD_W), BF16),
        scratch_shapes=[pltpu.VMEM((ROWS, s), I32), pltpu.VMEM((ROWS, s), F32)]
        + _attention_scratch(s),
        compiler_params=pltpu.CompilerParams(dimension_semantics=("arbitrary",) * 2,
                                             vmem_limit_bytes=VMEM_LIMIT),
        name="dsa_attention",
    )(iq, ik, iw, bq, bk, bv, bg, bias)


def _outproj_kernel(ao_ref, bo_ref, x_ref, w_ref, fw_ref, o_ref):
    y = jnp.dot(ao_ref[...], w_ref[0:HEAD_W, :], preferred_element_type=F32)
    y = y + jnp.dot(bo_ref[...], w_ref[HEAD_W:2 * HEAD_W, :], preferred_element_type=F32)
    z = x_ref[...] + y
    o_ref[...] = (z * lax.rsqrt(jnp.mean(z * z, axis=-1, keepdims=True) + EPS)) * fw_ref[...]


def _outproj(ao, bo, x2, w_out, final_w):
    n, d = x2.shape
    row = lambda w: pl.BlockSpec((PROJ_TM, w), lambda i: (i, 0))
    return pl.pallas_call(
        _outproj_kernel, grid=(n // PROJ_TM,),
        in_specs=[row(HEAD_W), row(HEAD_W), row(d), pl.BlockSpec(w_out.shape, lambda i: (0, 0)),
                  pl.BlockSpec((1, d), lambda i: (0, 0))],
        out_specs=row(d), out_shape=jax.ShapeDtypeStruct((n, d), F32),
        compiler_params=pltpu.CompilerParams(dimension_semantics=("arbitrary",),
                                             vmem_limit_bytes=VMEM_LIMIT),
        name="outproj",
    )(ao, bo, x2, w_out, final_w.reshape(1, d))


def kernel(x, norm_w, w_in, w_out, lambda_q1, lambda_k1, lambda_q2, lambda_k2, subln_w, rel_bias,
           final_norm_w):
    b, s, d = x.shape
    depth = norm_w.shape[0]
    assert depth == 1, "single-layer trunk"
    assert s % KV_TILE == 0 and (b * s) % PROJ_TM == 0
    layer = 0
    lambda_init = 0.8 - 0.6 * math.exp(-0.3 * layer)
    k_sel = min(TOPK_MAX, s // 4)

    w = w_in[layer]
    main = 8 * HEAD_W
    iq_w = w[:, main:main + IDX_HEADS * IDX_DIM]
    ik_w = w[:, main + IDX_HEADS * IDX_DIM:main + IDX_HEADS * IDX_DIM + IDX_DIM]
    iw_w = w[:, main + IDX_HEADS * IDX_DIM + IDX_DIM:]
    pad = jnp.zeros((d, LANES - IDX_HEADS), w.dtype)
    w_pad = jnp.concatenate([w[:, :main], iq_w, ik_w, ik_w, iw_w, pad], axis=1).astype(BF16)

    x2 = x.reshape(b * s, d)
    aq, ak, av, ag, bq, bk, bv, bg, iq, ik, iw = _inproj(x2, norm_w[layer], w_pad)
    r3 = lambda a: a.reshape(b, s, a.shape[-1])
    bias = _bias_tiles(rel_bias)

    vec = lambda a: a.reshape(1, -1)
    ao = _diff_attention(r3(aq), r3(ak), r3(av), r3(ag), bias[:A_HEADS],
                         vec(lambda_q1[layer]), vec(lambda_k1[layer]),
                         vec(lambda_q2[layer]), vec(lambda_k2[layer]), vec(subln_w[layer]),
                         lambda_init)
    bo = _dsa_attention(r3(iq), r3(ik), r3(iw), r3(bq), r3(bk), r3(bv), r3(bg),
                        bias[A_HEADS:], k_sel)

    out = _outproj(ao.reshape(b * s, -1), bo.reshape(b * s, -1), x2,
                   w_out[layer].astype(BF16), final_norm_w)
    return out.reshape(b, s, d)
```

```python
import functools
import math

import jax
import jax.numpy as jnp
from jax import lax
from jax.experimental import pallas as pl
from jax.experimental.pallas import tpu as pltpu

F32 = jnp.float32
BF16 = jnp.bfloat16
I32 = jnp.int32

EPS = 1e-6
CHUNK = 64
A_HEADS = 4
A_QK_DIM = 64
A_V_DIM = 128
B_HEADS = 8
B_HEAD_DIM = 64
IDX_HEADS = 4
IDX_DIM = 64
TOPK_MAX = 256
N_BUCKETS = 32
HEAD_W = 512
LANES = 128
ROWS = 128
KV_TILE = 256
N_UNITS = 8
PROJ_TM = 512
I16 = jnp.int16
SCAN_TILE = 512
PACK = 16
I16_MIN = -2 ** 15
NEG_INF = float("-inf")
VMEM_LIMIT = 56 * 1024 * 1024

_T5_THRESHOLDS = (12, 16, 23, 32, 46, 64, 91)


def _dot_nt(a, b):
    return lax.dot_general(a, b, (((1,), (1,)), ((), ())), preferred_element_type=F32)


def _resident(shape, index_map):
    return pl.BlockSpec(shape, index_map, pipeline_mode=pl.Buffered(1))


def _inproj_kernel(x_ref, nw_ref, w_ref, aq, ak, av, ag, bq, bk, bv, bg, iq, ik, iw,
                   *, q_scale, idx_scale):
    x = x_ref[...]
    ms = jnp.mean(x * x, axis=-1, keepdims=True)
    h = ((x * lax.rsqrt(ms + EPS)) * nw_ref[...]).astype(BF16)

    def proj(c):
        return jnp.dot(h, w_ref[:, c * HEAD_W:(c + 1) * HEAD_W], preferred_element_type=F32)

    aq[...] = (proj(0) * q_scale).astype(BF16)
    ak[...] = proj(1).astype(BF16)
    av[...] = proj(2).astype(BF16)
    ag[...] = proj(3)
    bq[...] = (proj(4) * q_scale).astype(BF16)
    bk[...] = proj(5).astype(BF16)
    bv[...] = proj(6).astype(BF16)
    bg[...] = proj(7)
    tail = proj(8)
    iq[...] = tail[:, 0:256].astype(BF16)
    ik[...] = tail[:, 256:384].astype(BF16)
    iw[...] = tail[:, 384:512] * idx_scale


def _inproj(x2, norm_w, w_pad):
    n, d = x2.shape
    grid = (n // PROJ_TM,)
    row = lambda w: pl.BlockSpec((PROJ_TM, w), lambda i: (i, 0))
    out_shape = ([jax.ShapeDtypeStruct((n, HEAD_W), BF16)] * 3 + [jax.ShapeDtypeStruct((n, HEAD_W), F32)]) * 2 + [
        jax.ShapeDtypeStruct((n, 256), BF16), jax.ShapeDtypeStruct((n, LANES), BF16),
        jax.ShapeDtypeStruct((n, LANES), F32)]
    out_specs = [row(HEAD_W)] * 8 + [row(256), row(LANES), row(LANES)]
    kern = functools.partial(_inproj_kernel, q_scale=A_QK_DIM ** -0.5,
                             idx_scale=(IDX_HEADS ** -0.5) * (IDX_DIM ** -0.5))
    return pl.pallas_call(
        kern, grid=grid,
        in_specs=[row(d), pl.BlockSpec((1, d), lambda i: (0, 0)),
                  pl.BlockSpec(w_pad.shape, lambda i: (0, 0))],
        out_specs=out_specs, out_shape=out_shape,
        compiler_params=pltpu.CompilerParams(dimension_semantics=("arbitrary",),
                                             vmem_limit_bytes=VMEM_LIMIT),
        name="inproj",
    )(x2, norm_w.reshape(1, d), w_pad)


def _bias_kernel(tab_ref, out_ref):
    h = pl.program_id(0)
    t = lax.broadcasted_iota(I32, (KV_TILE, KV_TILE), 0)
    s = lax.broadcasted_iota(I32, (KV_TILE, KV_TILE), 1)
    far = tab_ref[N_BUCKETS // 2 - 1, h]
    for kind in (0, 1):
        rel = s - t - KV_TILE * kind
        n = jnp.abs(rel)
        large = jnp.full_like(n, N_BUCKETS // 4)
        for thr in _T5_THRESHOLDS:
            large = large + (n >= thr).astype(I32)
        bucket = jnp.where(rel > 0, N_BUCKETS // 2, 0) + jnp.where(n < N_BUCKETS // 4, n, large)
        bias = jnp.zeros((KV_TILE, KV_TILE), F32)
        for b in range(N_BUCKETS):
            bias = jnp.where(bucket == b, tab_ref[b, h], bias)
        bias = bias - far
        if kind == 0:
            blocked = jnp.logical_and(h < A_HEADS, (s // CHUNK) > (t // CHUNK))
            bias = jnp.where(blocked, NEG_INF, bias)
        out_ref[kind] = bias


def _bias_tiles(rel_bias):
    nh = rel_bias.shape[1]
    return pl.pallas_call(
        _bias_kernel, grid=(nh,),
        in_specs=[pl.BlockSpec(memory_space=pltpu.SMEM)],
        out_specs=pl.BlockSpec((None, 2, KV_TILE, KV_TILE), lambda h: (h, 0, 0, 0)),
        out_shape=jax.ShapeDtypeStruct((nh, 2, KV_TILE, KV_TILE), F32),
        name="bias_tiles",
    )(rel_bias)


def _stack_masked_queries(q, qm_scr):
    low = lax.broadcasted_iota(I32, (1, LANES), 1) < LANES // 2
    for p in range(N_UNITS // 2):
        qp = q[:, p * LANES:(p + 1) * LANES]
        zero = jnp.zeros_like(qp)
        qm_scr[p, 0:ROWS, :] = jnp.where(low, qp, zero)
        qm_scr[p, ROWS:2 * ROWS, :] = jnp.where(low, zero, qp)


def _two_pass_attention(i, r0, qm_scr, k_ref, v_ref, bias_ref, bias_head, mask_scr,
                        s_scr, mx_scr, l_scr, acc_scr):
    n_tiles = i + 1
    n_far = jnp.maximum(i - 1, 0)
    mx_scr[...] = jnp.full(mx_scr.shape, NEG_INF, F32)
    l_scr[...] = jnp.zeros(l_scr.shape, F32)
    acc_scr[...] = jnp.zeros(acc_scr.shape, F32)

    def logits_pass(near):
        def body(j, carry):
            start = pl.multiple_of(j * KV_TILE, KV_TILE)
            add = None if mask_scr is None else mask_scr[:, pl.ds(start, KV_TILE)]
            for p in range(N_UNITS // 2):
                kc = k_ref[pl.ds(start, KV_TILE), p * LANES:(p + 1) * LANES]
                s2 = _dot_nt(qm_scr[p], kc)
                for half in range(2):
                    u = 2 * p + half
                    s = s2[half * ROWS:(half + 1) * ROWS]
                    if add is not None:
                        s = s + add
                    if near:
                        s = s + bias_ref[bias_head(u), i - j, pl.ds(r0, ROWS), :]
                    s_scr[u, :, pl.ds(start, KV_TILE)] = s
                    mx_scr[u] = jnp.maximum(mx_scr[u], jnp.maximum(s[:, :LANES], s[:, LANES:]))
            return carry
        return body

    lax.fori_loop(0, n_far, logits_pass(False), 0)
    lax.fori_loop(n_far, n_tiles, logits_pass(True), 0)

    for u in range(N_UNITS):
        m = jnp.max(mx_scr[u], axis=1, keepdims=True)
        mx_scr[u] = jnp.broadcast_to(m, (ROWS, LANES))

    def exp_pass(j, carry):
        start = pl.multiple_of(j * KV_TILE, KV_TILE)
        for u in range(N_UNITS):
            m = mx_scr[u]
            s = s_scr[u, :, pl.ds(start, KV_TILE)]
            p_lo = jnp.exp(s[:, :LANES] - m)
            p_hi = jnp.exp(s[:, LANES:] - m)
            l_scr[u] = l_scr[u] + (p_lo + p_hi)
            pb = jnp.concatenate([p_lo, p_hi], axis=1).astype(BF16)
            g = u // 2
            vc = v_ref[pl.ds(start, KV_TILE), g * LANES:(g + 1) * LANES]
            acc_scr[u] = acc_scr[u] + jnp.dot(pb, vc, preferred_element_type=F32)
        return carry

    lax.fori_loop(0, n_tiles, exp_pass, 0)


def _normalised(u, l_scr, acc_scr):
    return acc_scr[u] / jnp.sum(l_scr[u], axis=1, keepdims=True)


def _tree_sum(x):
    parts = [x[t] for t in range(x.shape[0])]
    while len(parts) > 1:
        parts = [parts[t] + parts[t + 1] for t in range(0, len(parts) - 1, 2)] + (
            [parts[-1]] if len(parts) % 2 else [])
    return parts[0]


def _silu(g):
    return g * (1.0 / (1.0 + jnp.exp(-g)))


def _attention_scratch(s):
    return [pltpu.VMEM((N_UNITS // 2, 2 * ROWS, LANES), BF16),
            pltpu.VMEM((N_UNITS, ROWS, s), F32),
            pltpu.VMEM((N_UNITS, ROWS, LANES), F32),
            pltpu.VMEM((N_UNITS, ROWS, LANES), F32),
            pltpu.VMEM((N_UNITS, ROWS, LANES), F32)]


def _diff_kernel(lq1, lk1, lq2, lk2, subw_ref, q_ref, k_ref, v_ref, g_ref, bias_ref, o_ref,
                 qm_scr, s_scr, mx_scr, l_scr, acc_scr, *, lambda_init):
    tb = pl.program_id(1)
    i = tb // 2
    r0 = pl.multiple_of((tb % 2) * ROWS, ROWS)
    lam = (jnp.exp(jnp.sum(lq1[...] * lk1[...], axis=-1, keepdims=True))
           - jnp.exp(jnp.sum(lq2[...] * lk2[...], axis=-1, keepdims=True)) + lambda_init)
    _stack_masked_queries(q_ref[...], qm_scr)
    _two_pass_attention(i, r0, qm_scr, k_ref, v_ref, bias_ref, lambda u: u // 2, None,
                        s_scr, mx_scr, l_scr, acc_scr)
    for h in range(A_HEADS):
        o = _normalised(2 * h, l_scr, acc_scr) - lam * _normalised(2 * h + 1, l_scr, acc_scr)
        y = (o * lax.rsqrt(jnp.mean(o * o, axis=-1, keepdims=True) + EPS)) * subw_ref[...]
        y = y * (1.0 - lambda_init)
        cols = slice(h * LANES, (h + 1) * LANES)
        o_ref[:, cols] = (y * _silu(g_ref[:, cols])).astype(BF16)


def _diff_attention(aq, ak, av, ag, bias, lq1, lk1, lq2, lk2, subln_w, lambda_init):
    b, s, _ = aq.shape
    grid = (b, s // ROWS)
    vec = lambda a: pl.BlockSpec((1, a.shape[-1]), lambda bi, t: (0, 0))
    rows = pl.BlockSpec((None, ROWS, HEAD_W), lambda bi, t: (bi, t, 0))
    full = _resident((None, s, HEAD_W), lambda bi, t: (bi, 0, 0))
    kern = functools.partial(_diff_kernel, lambda_init=lambda_init)
    return pl.pallas_call(
        kern, grid=grid,
        in_specs=[vec(lq1), vec(lk1), vec(lq2), vec(lk2), vec(subln_w), rows, full, full, rows,
                  _resident(bias.shape, lambda bi, t: (0, 0, 0, 0))],
        out_specs=rows,
        out_shape=jax.ShapeDtypeStruct((b, s, HEAD_W), BF16),
        scratch_shapes=_attention_scratch(s),
        compiler_params=pltpu.CompilerParams(dimension_semantics=("arbitrary",) * 2,
                                             vmem_limit_bytes=VMEM_LIMIT),
        name="diff_attention",
    )(lq1, lk1, lq2, lk2, subln_w, aq, ak, av, ag, bias)


def _dsa_kernel(iq_ref, ik_ref, iw_ref, q_ref, k_ref, v_ref, g_ref, bias_ref, o_ref,
                key_scr, hi_scr, lo_scr, iqm_scr, m_scr, qm_scr, s_scr, mx_scr, l_scr, acc_scr,
                *, k_sel):
    tb = pl.program_id(1)
    i = tb // 2
    n_tiles = i + 1
    n_scan = (n_tiles + 1) // 2
    r0 = pl.multiple_of((tb % 2) * ROWS, ROWS)
    lane = lax.broadcasted_iota(I32, (1, LANES), 1)
    low = lane < B_HEAD_DIM
    q_chunk = (tb * ROWS + lane) // CHUNK
    key_in_tile = lax.broadcasted_iota(I32, (KV_TILE, LANES), 0)

    iq = iq_ref[...]
    for hh in range(IDX_HEADS):
        pair = iq[:, (hh // 2) * LANES:(hh // 2 + 1) * LANES]
        iqm_scr[hh * ROWS:(hh + 1) * ROWS, :] = jnp.where(low == (hh % 2 == 0), pair,
                                                          jnp.zeros_like(pair))
    iw_t = iw_ref[...].T

    def score_tile(j, diagonal):
        start = pl.multiple_of(j * KV_TILE, KV_TILE)
        logits = _dot_nt(ik_ref[pl.ds(start, KV_TILE), :], iqm_scr[...])
        sc = jnp.zeros((KV_TILE, LANES), F32)
        for hh in range(IDX_HEADS):
            sc = sc + iw_t[hh:hh + 1, :] * jnp.maximum(logits[:, hh * ROWS:(hh + 1) * ROWS], 0.0)
        if diagonal:
            sc = jnp.where(((start + key_in_tile) // CHUNK) <= q_chunk, sc, NEG_INF)
        bits = lax.bitcast_convert_type(sc, I32)
        key = bits ^ (lax.shift_right_arithmetic(bits, 31) & 0x7FFFFFFF)
        key_scr[pl.ds(start, KV_TILE), :] = key
        hi_scr[pl.ds(start, KV_TILE), :] = lax.shift_right_arithmetic(key, 16).astype(I16)
        lo_scr[pl.ds(start, KV_TILE), :] = ((key & 0xFFFF) + I16_MIN).astype(I16)

    def far_scores(t, carry):
        score_tile(2 * t, False)
        score_tile(2 * t + 1, False)
        return carry

    lax.fori_loop(0, i // 2, far_scores, 0)

    @pl.when(i % 2 == 1)
    def _():
        score_tile(i - 1, False)

    score_tile(i, True)

    @pl.when(n_tiles % 2 == 1)
    def _():
        pad = pl.ds(pl.multiple_of(n_tiles * KV_TILE, KV_TILE), KV_TILE)
        hi_scr[pad, :] = jnp.full((KV_TILE, LANES), I16_MIN, I16)
        lo_scr[pad, :] = jnp.full((KV_TILE, LANES), I16_MIN, I16)

    def scan_count(ref, pred):
        def body(t, acc):
            blk = ref[pl.ds(pl.multiple_of(t * SCAN_TILE, SCAN_TILE), SCAN_TILE), :]
            hit = jnp.where(pred(blk.reshape(SCAN_TILE // PACK, PACK, LANES)),
                            jnp.int16(1), jnp.int16(0))
            return acc + _tree_sum(hit)
        acc = lax.fori_loop(0, n_scan, body, jnp.zeros((PACK, LANES), I16))
        return jnp.sum(acc.astype(I32), axis=0, keepdims=True)

    def splat16(v):
        return jnp.broadcast_to(v, (PACK, LANES)).astype(I16)

    def radix16(ref, want, c_start):
        def step(b, carry):
            thr, c_thr = carry
            cand = thr + lax.shift_left(jnp.int32(1), 15 - b)
            cand16 = splat16(cand)
            c = scan_count(ref, lambda blk: blk >= cand16)
            ok = c >= want
            return jnp.where(ok, cand, thr), jnp.where(ok, c, c_thr)
        return lax.fori_loop(0, 16, step, (jnp.full((1, LANES), I16_MIN, I32), c_start))

    n_keys = jnp.full((1, LANES), 1, I32) * (n_tiles * KV_TILE)
    thr_hi, c_ge_hi = radix16(hi_scr, k_sel, n_keys)

    hi16 = splat16(thr_hi)

    def split_body(t, acc):
        rows = pl.ds(pl.multiple_of(t * SCAN_TILE, SCAN_TILE), SCAN_TILE)
        h = hi_scr[rows, :].reshape(SCAN_TILE // PACK, PACK, LANES)
        lo = lo_scr[rows, :].reshape(SCAN_TILE // PACK, PACK, LANES)
        lo_scr[rows, :] = jnp.where(h == hi16, lo, jnp.int16(I16_MIN)).reshape(SCAN_TILE, LANES)
        return acc + _tree_sum(jnp.where(h > hi16, jnp.int16(1), jnp.int16(0)))

    above = lax.fori_loop(0, n_scan, split_body, jnp.zeros((PACK, LANES), I16))
    c_gt_hi = jnp.sum(above.astype(I32), axis=0, keepdims=True)
    thr_lo, c_lo = radix16(lo_scr, k_sel - c_gt_hi, c_ge_hi - c_gt_hi)
    thr = thr_hi * 65536 + (thr_lo - I16_MIN)
    c_thr = c_gt_hi + c_lo

    def tile_count(pred):
        def body(j, acc):
            start = pl.multiple_of(j * KV_TILE, KV_TILE)
            hit = jnp.where(pred(key_scr[pl.ds(start, KV_TILE), :], start + key_in_tile), 1, 0)
            return acc + jnp.sum(hit.reshape(KV_TILE // 8, 8, LANES), axis=0)
        acc = lax.fori_loop(0, n_tiles, body, jnp.zeros((8, LANES), I32))
        return jnp.sum(acc, axis=0, keepdims=True)

    def tie_limit():
        need = k_sel - tile_count(lambda kc, idx: kc > thr)

        def idx_step(b, lim):
            cand = lim + lax.shift_left(jnp.int32(1), 11 - b)
            f = tile_count(lambda kc, idx: jnp.logical_and(kc == thr, idx < cand))
            return jnp.where(f < need, cand, lim)

        return lax.fori_loop(0, 12, idx_step, jnp.zeros((1, LANES), I32))

    lim = lax.cond(jnp.max(c_thr) > k_sel, tie_limit,
                   lambda: jnp.full((1, LANES), 2 ** 30, I32))

    def mask_tile(j, diagonal):
        start = pl.multiple_of(j * KV_TILE, KV_TILE)
        kc = key_scr[pl.ds(start, KV_TILE), :]
        idx = start + key_in_tile
        sel = jnp.logical_or(kc > thr, jnp.logical_and(kc == thr, idx <= lim))
        if diagonal:
            sel = jnp.logical_and(sel, (idx // CHUNK) <= q_chunk)
        m_scr[:, pl.ds(start, KV_TILE)] = jnp.where(sel, 0.0, NEG_INF).T

    def far_mask(j, carry):
        mask_tile(j, False)
        return carry

    lax.fori_loop(0, i, far_mask, 0)
    mask_tile(i, True)

    _stack_masked_queries(q_ref[...], qm_scr)
    _two_pass_attention(i, r0, qm_scr, k_ref, v_ref, bias_ref, lambda u: u, m_scr,
                        s_scr, mx_scr, l_scr, acc_scr)
    for p in range(B_HEADS // 2):
        o = jnp.where(low, _normalised(2 * p, l_scr, acc_scr), _normalised(2 * p + 1, l_scr, acc_scr))
        cols = slice(p * LANES, (p + 1) * LANES)
        o_ref[:, cols] = (o * _silu(g_ref[:, cols])).astype(BF16)


def _dsa_attention(iq, ik, iw, bq, bk, bv, bg, bias, k_sel):
    b, s, _ = bq.shape
    grid = (b, s // ROWS)
    rows = lambda w: pl.BlockSpec((None, ROWS, w), lambda bi, t: (bi, t, 0))
    full = lambda w: _resident((None, s, w), lambda bi, t: (bi, 0, 0))
    kern = functools.partial(_dsa_kernel, k_sel=k_sel)
    return pl.pallas_call(
        kern, grid=grid,
        in_specs=[rows(256), full(LANES), rows(LANES), rows(HEAD_W), full(HEAD_W), full(HEAD_W),
                  rows(HEAD_W), _resident(bias.shape, lambda bi, t: (0, 0, 0, 0))],
        out_specs=rows(HEAD_W),
        out_shape=jax.ShapeDtypeStruct((b, s, HEAD_W), BF16),
        scratch_shapes=[pltpu.VMEM((s, LANES), I32),
                        pltpu.VMEM((s, LANES), I16), pltpu.VMEM((s, LANES), I16),
                        pltpu.VMEM((IDX_HEADS * ROWS, LANES), BF16),
                        pltpu.VMEM((ROWS, s), F32)]
        + _attention_scratch(s),
        compiler_params=pltpu.CompilerParams(dimension_semantics=("arbitrary",) * 2,
                                             vmem_limit_bytes=VMEM_LIMIT),
        name="dsa_attention",
    )(iq, ik, iw, bq, bk, bv, bg, bias)


def _outproj_kernel(ao_ref, bo_ref, x_ref, w_ref, fw_ref, o_ref):
    y = jnp.dot(ao_ref[...], w_ref[0:HEAD_W, :], preferred_element_type=F32)
    y = y + jnp.dot(bo_ref[...], w_ref[HEAD_W:2 * HEAD_W, :], preferred_element_type=F32)
    z = x_ref[...] + y
    o_ref[...] = (z * lax.rsqrt(jnp.mean(z * z, axis=-1, keepdims=True) + EPS)) * fw_ref[...]


def _outproj(ao, bo, x2, w_out, final_w):
    n, d = x2.shape
    row = lambda w: pl.BlockSpec((PROJ_TM, w), lambda i: (i, 0))
    return pl.pallas_call(
        _outproj_kernel, grid=(n // PROJ_TM,),
        in_specs=[row(HEAD_W), row(HEAD_W), row(d), pl.BlockSpec(w_out.shape, lambda i: (0, 0)),
                  pl.BlockSpec((1, d), lambda i: (0, 0))],
        out_specs=row(d), out_shape=jax.ShapeDtypeStruct((n, d), F32),
        compiler_params=pltpu.CompilerParams(dimension_semantics=("arbitrary",),
                                             vmem_limit_bytes=VMEM_LIMIT),
        name="outproj",
    )(ao, bo, x2, w_out, final_w.reshape(1, d))


def kernel(x, norm_w, w_in, w_out, lambda_q1, lambda_k1, lambda_q2, lambda_k2, subln_w, rel_bias,
           final_norm_w):
    b, s, d = x.shape
    depth = norm_w.shape[0]
    assert depth == 1, "single-layer trunk"
    assert s % KV_TILE == 0 and (b * s) % PROJ_TM == 0
    layer = 0
    lambda_init = 0.8 - 0.6 * math.exp(-0.3 * layer)
    k_sel = min(TOPK_MAX, s // 4)

    w = w_in[layer]
    main = 8 * HEAD_W
    iq_w = w[:, main:main + IDX_HEADS * IDX_DIM]
    ik_w = w[:, main + IDX_HEADS * IDX_DIM:main + IDX_HEADS * IDX_DIM + IDX_DIM]
    iw_w = w[:, main + IDX_HEADS * IDX_DIM + IDX_DIM:]
    pad = jnp.zeros((d, LANES - IDX_HEADS), w.dtype)
    w_pad = jnp.concatenate([w[:, :main], iq_w, ik_w, ik_w, iw_w, pad], axis=1).astype(BF16)

    x2 = x.reshape(b * s, d)
    aq, ak, av, ag, bq, bk, bv, bg, iq, ik, iw = _inproj(x2, norm_w[layer], w_pad)
    r3 = lambda a: a.reshape(b, s, a.shape[-1])
    bias = _bias_tiles(rel_bias)

    vec = lambda a: a.reshape(1, -1)
    ao = _diff_attention(r3(aq), r3(ak), r3(av), r3(ag), bias[:A_HEADS],
                         vec(lambda_q1[layer]), vec(lambda_k1[layer]),
                         vec(lambda_q2[layer]), vec(lambda_k2[layer]), vec(subln_w[layer]),
                         lambda_init)
    bo = _dsa_attention(r3(iq), r3(ik), r3(iw), r3(bq), r3(bk), r3(bv), r3(bg),
                        bias[A_HEADS:], k_sel)

    out = _outproj(ao.reshape(b * s, -1), bo.reshape(b * s, -1), x2,
                   w_out[layer].astype(BF16), final_norm_w)
    return out.reshape(b, s, d)
```

```python
import functools
import math

import jax
import jax.numpy as jnp
from jax import lax
from jax.experimental import pallas as pl
from jax.experimental.pallas import tpu as pltpu

F32 = jnp.float32
BF16 = jnp.bfloat16
I32 = jnp.int32

EPS = 1e-6
CHUNK = 64
A_HEADS = 4
A_QK_DIM = 64
A_V_DIM = 128
B_HEADS = 8
B_HEAD_DIM = 64
IDX_HEADS = 4
IDX_DIM = 64
TOPK_MAX = 256
N_BUCKETS = 32
HEAD_W = 512
LANES = 128
ROWS = 128
KV_TILE = 256
N_UNITS = 8
PROJ_TM = 512
SCAN_TILE = 512
INT_MIN = -2 ** 31
KEY_NEG_INF = INT_MIN + 0x7FFFFF
NEG_INF = float("-inf")
MASKED_LOGIT = -(2.0 ** 120)
VMEM_LIMIT = 56 * 1024 * 1024

_T5_THRESHOLDS = (12, 16, 23, 32, 46, 64, 91)


def _dot_nt(a, b):
    return lax.dot_general(a, b, (((1,), (1,)), ((), ())), preferred_element_type=F32)


def _resident(shape, index_map):
    return pl.BlockSpec(shape, index_map, pipeline_mode=pl.Buffered(1))


def _inproj_kernel(x_ref, nw_ref, w_ref, aq, ak, av, ag, bq, bk, bv, bg, iq, ik, iw,
                   *, q_scale, idx_scale):
    x = x_ref[...]
    ms = jnp.mean(x * x, axis=-1, keepdims=True)
    h = ((x * lax.rsqrt(ms + EPS)) * nw_ref[...]).astype(BF16)

    def proj(c):
        return jnp.dot(h, w_ref[:, c * HEAD_W:(c + 1) * HEAD_W], preferred_element_type=F32)

    aq[...] = (proj(0) * q_scale).astype(BF16)
    ak[...] = proj(1).astype(BF16)
    av[...] = proj(2).astype(BF16)
    ag[...] = proj(3)
    bq[...] = (proj(4) * q_scale).astype(BF16)
    bk[...] = proj(5).astype(BF16)
    bv[...] = proj(6).astype(BF16)
    bg[...] = proj(7)
    tail = proj(8)
    iq[...] = tail[:, 0:256].astype(BF16)
    ik[...] = tail[:, 256:384].astype(BF16)
    iw[...] = tail[:, 384:512] * idx_scale


def _inproj(x2, norm_w, w_pad):
    n, d = x2.shape
    grid = (n // PROJ_TM,)
    row = lambda w: pl.BlockSpec((PROJ_TM, w), lambda i: (i, 0))
    out_shape = ([jax.ShapeDtypeStruct((n, HEAD_W), BF16)] * 3 + [jax.ShapeDtypeStruct((n, HEAD_W), F32)]) * 2 + [
        jax.ShapeDtypeStruct((n, 256), BF16), jax.ShapeDtypeStruct((n, LANES), BF16),
        jax.ShapeDtypeStruct((n, LANES), F32)]
    out_specs = [row(HEAD_W)] * 8 + [row(256), row(LANES), row(LANES)]
    kern = functools.partial(_inproj_kernel, q_scale=A_QK_DIM ** -0.5,
                             idx_scale=(IDX_HEADS ** -0.5) * (IDX_DIM ** -0.5))
    return pl.pallas_call(
        kern, grid=grid,
        in_specs=[row(d), pl.BlockSpec((1, d), lambda i: (0, 0)),
                  pl.BlockSpec(w_pad.shape, lambda i: (0, 0))],
        out_specs=out_specs, out_shape=out_shape,
        compiler_params=pltpu.CompilerParams(dimension_semantics=("arbitrary",),
                                             vmem_limit_bytes=VMEM_LIMIT),
        name="inproj",
    )(x2, norm_w.reshape(1, d), w_pad)


def _bias_kernel(tab_ref, out_ref):
    h = pl.program_id(0)
    t = lax.broadcasted_iota(I32, (KV_TILE, KV_TILE), 0)
    s = lax.broadcasted_iota(I32, (KV_TILE, KV_TILE), 1)
    far = tab_ref[N_BUCKETS // 2 - 1, h]
    for kind in (0, 1):
        rel = s - t - KV_TILE * kind
        n = jnp.abs(rel)
        large = jnp.full_like(n, N_BUCKETS // 4)
        for thr in _T5_THRESHOLDS:
            large = large + (n >= thr).astype(I32)
        bucket = jnp.where(rel > 0, N_BUCKETS // 2, 0) + jnp.where(n < N_BUCKETS // 4, n, large)
        bias = jnp.zeros((KV_TILE, KV_TILE), F32)
        for b in range(N_BUCKETS):
            bias = jnp.where(bucket == b, tab_ref[b, h], bias)
        bias = bias - far
        if kind == 0:
            blocked = jnp.logical_and(h < A_HEADS, (s // CHUNK) > (t // CHUNK))
            bias = jnp.where(blocked, NEG_INF, bias)
        out_ref[kind] = bias


def _bias_tiles(rel_bias):
    nh = rel_bias.shape[1]
    return pl.pallas_call(
        _bias_kernel, grid=(nh,),
        in_specs=[pl.BlockSpec(memory_space=pltpu.SMEM)],
        out_specs=pl.BlockSpec((None, 2, KV_TILE, KV_TILE), lambda h: (h, 0, 0, 0)),
        out_shape=jax.ShapeDtypeStruct((nh, 2, KV_TILE, KV_TILE), F32),
        name="bias_tiles",
    )(rel_bias)


def _stack_masked_queries(q, qm_scr):
    low = lax.broadcasted_iota(I32, (1, LANES), 1) < LANES // 2
    for p in range(N_UNITS // 2):
        qp = q[:, p * LANES:(p + 1) * LANES]
        zero = jnp.zeros_like(qp)
        qm_scr[p, 0:ROWS, 0:LANES] = jnp.where(low, qp, zero)
        qm_scr[p, ROWS:2 * ROWS, 0:LANES] = jnp.where(low, zero, qp)
    if qm_scr.shape[-1] == 2 * LANES:
        hit = (lax.broadcasted_iota(I32, (ROWS, LANES), 0)
               == lax.broadcasted_iota(I32, (ROWS, LANES), 1))
        eye = jnp.where(hit, MASKED_LOGIT, 0.0).astype(BF16)
        for p in range(N_UNITS // 2):
            qm_scr[p, 0:ROWS, LANES:2 * LANES] = eye
            qm_scr[p, ROWS:2 * ROWS, LANES:2 * LANES] = eye


def _two_pass_attention(i, r0, qm_scr, k_ref, v_ref, bias_ref, bias_head, unsel_scr,
                        s_scr, mx_scr, l_scr, acc_scr):
    n_tiles = i + 1
    n_far = jnp.maximum(i - 1, 0)
    mx_scr[...] = jnp.full(mx_scr.shape, NEG_INF, F32)
    l_scr[...] = jnp.zeros(l_scr.shape, F32)
    acc_scr[...] = jnp.zeros(acc_scr.shape, F32)

    def logits_pass(near):
        def body(j, carry):
            start = pl.multiple_of(j * KV_TILE, KV_TILE)
            for p in range(N_UNITS // 2):
                kc = k_ref[pl.ds(start, KV_TILE), p * LANES:(p + 1) * LANES]
                if unsel_scr is not None:
                    kc = jnp.concatenate([kc, unsel_scr[pl.ds(start, KV_TILE), :]], axis=1)
                s2 = _dot_nt(qm_scr[p], kc)
                for half in range(2):
                    u = 2 * p + half
                    s = s2[half * ROWS:(half + 1) * ROWS]
                    if near:
                        s = s + bias_ref[bias_head(u), i - j, pl.ds(r0, ROWS), :]
                    s_scr[u, :, pl.ds(start, KV_TILE)] = s
                    mx_scr[u] = jnp.maximum(mx_scr[u], jnp.maximum(s[:, :LANES], s[:, LANES:]))
            return carry
        return body

    lax.fori_loop(0, n_far, logits_pass(False), 0)
    lax.fori_loop(n_far, n_tiles, logits_pass(True), 0)

    for u in range(N_UNITS):
        m = jnp.max(mx_scr[u], axis=1, keepdims=True)
        mx_scr[u] = jnp.broadcast_to(m, (ROWS, LANES))

    def exp_pass(j, carry):
        start = pl.multiple_of(j * KV_TILE, KV_TILE)
        for u in range(N_UNITS):
            m = mx_scr[u]
            s = s_scr[u, :, pl.ds(start, KV_TILE)]
            p_lo = jnp.exp(s[:, :LANES] - m)
            p_hi = jnp.exp(s[:, LANES:] - m)
            l_scr[u] = l_scr[u] + (p_lo + p_hi)
            pb = jnp.concatenate([p_lo, p_hi], axis=1).astype(BF16)
            g = u // 2
            vc = v_ref[pl.ds(start, KV_TILE), g * LANES:(g + 1) * LANES]
            acc_scr[u] = acc_scr[u] + jnp.dot(pb, vc, preferred_element_type=F32)
        return carry

    lax.fori_loop(0, n_tiles, exp_pass, 0)


def _normalised(u, l_scr, acc_scr):
    return acc_scr[u] / jnp.sum(l_scr[u], axis=1, keepdims=True)


def _tree_sum(x):
    parts = [x[t] for t in range(x.shape[0])]
    while len(parts) > 1:
        parts = [parts[t] + parts[t + 1] for t in range(0, len(parts) - 1, 2)] + (
            [parts[-1]] if len(parts) % 2 else [])
    return parts[0]


def _silu(g):
    return g * (1.0 / (1.0 + jnp.exp(-g)))


def _attention_scratch(s, q_width=LANES):
    return [pltpu.VMEM((N_UNITS // 2, 2 * ROWS, q_width), BF16),
            pltpu.VMEM((N_UNITS, ROWS, s), F32),
            pltpu.VMEM((N_UNITS, ROWS, LANES), F32),
            pltpu.VMEM((N_UNITS, ROWS, LANES), F32),
            pltpu.VMEM((N_UNITS, ROWS, LANES), F32)]


def _diff_kernel(lq1, lk1, lq2, lk2, subw_ref, q_ref, k_ref, v_ref, g_ref, bias_ref, o_ref,
                 qm_scr, s_scr, mx_scr, l_scr, acc_scr, *, lambda_init):
    tb = pl.program_id(1)
    i = tb // 2
    r0 = pl.multiple_of((tb % 2) * ROWS, ROWS)
    lam = (jnp.exp(jnp.sum(lq1[...] * lk1[...], axis=-1, keepdims=True))
           - jnp.exp(jnp.sum(lq2[...] * lk2[...], axis=-1, keepdims=True)) + lambda_init)
    _stack_masked_queries(q_ref[...], qm_scr)
    _two_pass_attention(i, r0, qm_scr, k_ref, v_ref, bias_ref, lambda u: u // 2, None,
                        s_scr, mx_scr, l_scr, acc_scr)
    for h in range(A_HEADS):
        o = _normalised(2 * h, l_scr, acc_scr) - lam * _normalised(2 * h + 1, l_scr, acc_scr)
        y = (o * lax.rsqrt(jnp.mean(o * o, axis=-1, keepdims=True) + EPS)) * subw_ref[...]
        y = y * (1.0 - lambda_init)
        cols = slice(h * LANES, (h + 1) * LANES)
        o_ref[:, cols] = (y * _silu(g_ref[:, cols])).astype(BF16)


def _diff_attention(aq, ak, av, ag, bias, lq1, lk1, lq2, lk2, subln_w, lambda_init):
    b, s, _ = aq.shape
    grid = (b, s // ROWS)
    vec = lambda a: pl.BlockSpec((1, a.shape[-1]), lambda bi, t: (0, 0))
    rows = pl.BlockSpec((None, ROWS, HEAD_W), lambda bi, t: (bi, t, 0))
    full = _resident((None, s, HEAD_W), lambda bi, t: (bi, 0, 0))
    kern = functools.partial(_diff_kernel, lambda_init=lambda_init)
    return pl.pallas_call(
        kern, grid=grid,
        in_specs=[vec(lq1), vec(lk1), vec(lq2), vec(lk2), vec(subln_w), rows, full, full, rows,
                  _resident(bias.shape, lambda bi, t: (0, 0, 0, 0))],
        out_specs=rows,
        out_shape=jax.ShapeDtypeStruct((b, s, HEAD_W), BF16),
        scratch_shapes=_attention_scratch(s),
        compiler_params=pltpu.CompilerParams(dimension_semantics=("arbitrary",) * 2,
                                             vmem_limit_bytes=VMEM_LIMIT),
        name="diff_attention",
    )(lq1, lk1, lq2, lk2, subln_w, aq, ak, av, ag, bias)


def _dsa_kernel(iq_ref, ik_ref, iw_ref, q_ref, k_ref, v_ref, g_ref, bias_ref, o_ref,
                sc_scr, iqm_scr, unsel_scr, qm_scr, s_scr, mx_scr, l_scr, acc_scr, *, k_sel):
    tb = pl.program_id(1)
    i = tb // 2
    n_tiles = i + 1
    n_scan = (n_tiles + 1) // 2
    r0 = pl.multiple_of((tb % 2) * ROWS, ROWS)
    lane = lax.broadcasted_iota(I32, (1, LANES), 1)
    low = lane < B_HEAD_DIM
    q_chunk = (tb * ROWS + lane) // CHUNK
    key_in_tile = lax.broadcasted_iota(I32, (KV_TILE, LANES), 0)

    iq = iq_ref[...]
    for hh in range(IDX_HEADS):
        pair = iq[:, (hh // 2) * LANES:(hh // 2 + 1) * LANES]
        iqm_scr[hh * ROWS:(hh + 1) * ROWS, :] = jnp.where(low == (hh % 2 == 0), pair,
                                                          jnp.zeros_like(pair))
    iw_t = iw_ref[...].T

    def score_tile(j, diagonal):
        start = pl.multiple_of(j * KV_TILE, KV_TILE)
        logits = _dot_nt(ik_ref[pl.ds(start, KV_TILE), :], iqm_scr[...])
        sc = jnp.zeros((KV_TILE, LANES), F32)
        for hh in range(IDX_HEADS):
            sc = sc + iw_t[hh:hh + 1, :] * jnp.maximum(logits[:, hh * ROWS:(hh + 1) * ROWS], 0.0)
        if diagonal:
            sc = jnp.where(((start + key_in_tile) // CHUNK) <= q_chunk, sc, NEG_INF)
        sc_scr[pl.ds(start, KV_TILE), :] = sc

    def far_scores(t, carry):
        score_tile(2 * t, False)
        score_tile(2 * t + 1, False)
        return carry

    lax.fori_loop(0, i // 2, far_scores, 0)

    @pl.when(i % 2 == 1)
    def _():
        score_tile(i - 1, False)

    score_tile(i, True)

    @pl.when(n_tiles % 2 == 1)
    def _():
        pad = pl.ds(pl.multiple_of(n_tiles * KV_TILE, KV_TILE), KV_TILE)
        sc_scr[pad, :] = jnp.full((KV_TILE, LANES), NEG_INF, F32)

    def key_to_float(key):
        bits = key ^ (lax.shift_right_arithmetic(key, 31) & 0x7FFFFFFF)
        return lax.bitcast_convert_type(bits, F32)

    def scan_count(pred):
        def body(t, acc):
            blk = sc_scr[pl.ds(pl.multiple_of(t * SCAN_TILE, SCAN_TILE), SCAN_TILE), :]
            hit = jnp.where(pred(blk.reshape(SCAN_TILE // 8, 8, LANES)), 1.0, 0.0)
            return acc + _tree_sum(hit)
        acc = lax.fori_loop(0, n_scan, body, jnp.zeros((8, LANES), F32))
        return jnp.sum(acc, axis=0, keepdims=True)

    kf = float(k_sel)

    def bit_step(b, carry):
        key, c_key = carry
        cand = key + lax.shift_left(jnp.int32(1), 31 - b)
        cand_f = key_to_float(cand)
        c = scan_count(lambda blk: blk >= cand_f)
        ok = jnp.logical_or(c >= kf, cand <= KEY_NEG_INF)
        return jnp.where(ok, cand, key), jnp.where(ok, c, c_key)

    n_keys = jnp.full((1, LANES), 1.0, F32) * (n_tiles * KV_TILE).astype(F32)
    thr_key, c_thr = lax.fori_loop(0, 32, bit_step,
                                   (jnp.full((1, LANES), INT_MIN, I32), n_keys))
    thr = key_to_float(thr_key)

    def tile_count(pred):
        def body(j, acc):
            start = pl.multiple_of(j * KV_TILE, KV_TILE)
            hit = jnp.where(pred(sc_scr[pl.ds(start, KV_TILE), :], start + key_in_tile), 1.0, 0.0)
            return acc + _tree_sum(hit.reshape(KV_TILE // 8, 8, LANES))
        acc = lax.fori_loop(0, n_tiles, body, jnp.zeros((8, LANES), F32))
        return jnp.sum(acc, axis=0, keepdims=True)

    def tie_limit():
        need = kf - tile_count(lambda kc, idx: kc > thr)

        def idx_step(b, lim):
            cand = lim + lax.shift_left(jnp.int32(1), 11 - b)
            f = tile_count(lambda kc, idx: jnp.logical_and(kc == thr, idx < cand))
            return jnp.where(f < need, cand, lim)

        return lax.fori_loop(0, 12, idx_step, jnp.zeros((1, LANES), I32))

    lim = lax.cond(jnp.max(c_thr) > kf, tie_limit,
                   lambda: jnp.full((1, LANES), 2 ** 30, I32))

    def mask_tile(j, diagonal):
        start = pl.multiple_of(j * KV_TILE, KV_TILE)
        kc = sc_scr[pl.ds(start, KV_TILE), :]
        idx = start + key_in_tile
        sel = jnp.logical_or(kc > thr, jnp.logical_and(kc == thr, idx <= lim))
        if diagonal:
            sel = jnp.logical_and(sel, (idx // CHUNK) <= q_chunk)
        unsel_scr[pl.ds(start, KV_TILE), :] = jnp.where(sel, 0.0, 1.0).astype(BF16)

    def far_mask(j, carry):
        mask_tile(j, False)
        return carry

    lax.fori_loop(0, i, far_mask, 0)
    mask_tile(i, True)

    _stack_masked_queries(q_ref[...], qm_scr)
    _two_pass_attention(i, r0, qm_scr, k_ref, v_ref, bias_ref, lambda u: u, unsel_scr,
                        s_scr, mx_scr, l_scr, acc_scr)
    for p in range(B_HEADS // 2):
        o = jnp.where(low, _normalised(2 * p, l_scr, acc_scr), _normalised(2 * p + 1, l_scr, acc_scr))
        cols = slice(p * LANES, (p + 1) * LANES)
        o_ref[:, cols] = (o * _silu(g_ref[:, cols])).astype(BF16)


def _dsa_attention(iq, ik, iw, bq, bk, bv, bg, bias, k_sel):
    b, s, _ = bq.shape
    grid = (b, s // ROWS)
    rows = lambda w: pl.BlockSpec((None, ROWS, w), lambda bi, t: (bi, t, 0))
    full = lambda w: _resident((None, s, w), lambda bi, t: (bi, 0, 0))
    kern = functools.partial(_dsa_kernel, k_sel=k_sel)
    return pl.pallas_call(
        kern, grid=grid,
        in_specs=[rows(256), full(LANES), rows(LANES), rows(HEAD_W), full(HEAD_W), full(HEAD_W),
                  rows(HEAD_W), _resident(bias.shape, lambda bi, t: (0, 0, 0, 0))],
        out_specs=rows(HEAD_W),
        out_shape=jax.ShapeDtypeStruct((b, s, HEAD_W), BF16),
        scratch_shapes=[pltpu.VMEM((s, LANES), F32),
                        pltpu.VMEM((IDX_HEADS * ROWS, LANES), BF16),
                        pltpu.VMEM((s, LANES), BF16)]
        + _attention_scratch(s, 2 * LANES),
        compiler_params=pltpu.CompilerParams(dimension_semantics=("arbitrary",) * 2,
                                             vmem_limit_bytes=VMEM_LIMIT),
        name="dsa_attention",
    )(iq, ik, iw, bq, bk, bv, bg, bias)


def _outproj_kernel(ao_ref, bo_ref, x_ref, w_ref, fw_ref, o_ref):
    y = jnp.dot(ao_ref[...], w_ref[0:HEAD_W, :], preferred_element_type=F32)
    y = y + jnp.dot(bo_ref[...], w_ref[HEAD_W:2 * HEAD_W, :], preferred_element_type=F32)
    z = x_ref[...] + y
    o_ref[...] = (z * lax.rsqrt(jnp.mean(z * z, axis=-1, keepdims=True) + EPS)) * fw_ref[...]


def _outproj(ao, bo, x2, w_out, final_w):
    n, d = x2.shape
    row = lambda w: pl.BlockSpec((PROJ_TM, w), lambda i: (i, 0))
    return pl.pallas_call(
        _outproj_kernel, grid=(n // PROJ_TM,),
        in_specs=[row(HEAD_W), row(HEAD_W), row(d), pl.BlockSpec(w_out.shape, lambda i: (0, 0)),
                  pl.BlockSpec((1, d), lambda i: (0, 0))],
        out_specs=row(d), out_shape=jax.ShapeDtypeStruct((n, d), F32),
        compiler_params=pltpu.CompilerParams(dimension_semantics=("arbitrary",),
                                             vmem_limit_bytes=VMEM_LIMIT),
        name="outproj",
    )(ao, bo, x2, w_out, final_w.reshape(1, d))


def kernel(x, norm_w, w_in, w_out, lambda_q1, lambda_k1, lambda_q2, lambda_k2, subln_w, rel_bias,
           final_norm_w):
    b, s, d = x.shape
    depth = norm_w.shape[0]
    assert depth == 1, "single-layer trunk"
    assert s % KV_TILE == 0 and (b * s) % PROJ_TM == 0
    layer = 0
    lambda_init = 0.8 - 0.6 * math.exp(-0.3 * layer)
    k_sel = min(TOPK_MAX, s // 4)

    w = w_in[layer]
    main = 8 * HEAD_W
    iq_w = w[:, main:main + IDX_HEADS * IDX_DIM]
    ik_w = w[:, main + IDX_HEADS * IDX_DIM:main + IDX_HEADS * IDX_DIM + IDX_DIM]
    iw_w = w[:, main + IDX_HEADS * IDX_DIM + IDX_DIM:]
    pad = jnp.zeros((d, LANES - IDX_HEADS), w.dtype)
    w_pad = jnp.concatenate([w[:, :main], iq_w, ik_w, ik_w, iw_w, pad], axis=1).astype(BF16)

    x2 = x.reshape(b * s, d)
    aq, ak, av, ag, bq, bk, bv, bg, iq, ik, iw = _inproj(x2, norm_w[layer], w_pad)
    r3 = lambda a: a.reshape(b, s, a.shape[-1])
    bias = _bias_tiles(rel_bias)

    vec = lambda a: a.reshape(1, -1)
    ao = _diff_attention(r3(aq), r3(ak), r3(av), r3(ag), bias[:A_HEADS],
                         vec(lambda_q1[layer]), vec(lambda_k1[layer]),
                         vec(lambda_q2[layer]), vec(lambda_k2[layer]), vec(subln_w[layer]),
                         lambda_init)
    bo = _dsa_attention(r3(iq), r3(ik), r3(iw), r3(bq), r3(bk), r3(bv), r3(bg),
                        bias[A_HEADS:], k_sel)

    out = _outproj(ao.reshape(b * s, -1), bo.reshape(b * s, -1), x2,
                   w_out[layer].astype(BF16), final_norm_w)
    return out.reshape(b, s, d)
```

```python
import functools
import math

import jax
import jax.numpy as jnp
from jax import lax
from jax.experimental import pallas as pl
from jax.experimental.pallas import tpu as pltpu

F32 = jnp.float32
BF16 = jnp.bfloat16
I32 = jnp.int32

EPS = 1e-6
CHUNK = 64
A_HEADS = 4
A_QK_DIM = 64
A_V_DIM = 128
B_HEADS = 8
B_HEAD_DIM = 64
IDX_HEADS = 4
IDX_DIM = 64
TOPK_MAX = 256
N_BUCKETS = 32
HEAD_W = 512
LANES = 128
ROWS = 128
KV_TILE = 256
WIDE_TILE = 2 * KV_TILE
N_UNITS = 8
PROJ_TM = 512
SCAN_TILE = 512
INT_MIN = -2 ** 31
KEY_NEG_INF = INT_MIN + 0x7FFFFF
NEG_INF = float("-inf")
MASKED_LOGIT = -(2.0 ** 120)
VMEM_LIMIT = 56 * 1024 * 1024

_T5_THRESHOLDS = (12, 16, 23, 32, 46, 64, 91)


def _dot_nt(a, b):
    return lax.dot_general(a, b, (((1,), (1,)), ((), ())), preferred_element_type=F32)


def _resident(shape, index_map):
    return pl.BlockSpec(shape, index_map, pipeline_mode=pl.Buffered(1))


def _inproj_kernel(x_ref, nw_ref, w_ref, aq, ak, av, ag, bq, bk, bv, bg, iq, ik, iw,
                   *, q_scale, idx_scale):
    x = x_ref[...]
    ms = jnp.mean(x * x, axis=-1, keepdims=True)
    h = ((x * lax.rsqrt(ms + EPS)) * nw_ref[...]).astype(BF16)

    def proj(c):
        return jnp.dot(h, w_ref[:, c * HEAD_W:(c + 1) * HEAD_W], preferred_element_type=F32)

    aq[...] = (proj(0) * q_scale).astype(BF16)
    ak[...] = proj(1).astype(BF16)
    av[...] = proj(2).astype(BF16)
    ag[...] = proj(3)
    bq[...] = (proj(4) * q_scale).astype(BF16)
    bk[...] = proj(5).astype(BF16)
    bv[...] = proj(6).astype(BF16)
    bg[...] = proj(7)
    tail = proj(8)
    iq[...] = tail[:, 0:256].astype(BF16)
    ik[...] = tail[:, 256:384].astype(BF16)
    iw[...] = tail[:, 384:512] * idx_scale


def _inproj(x2, norm_w, w_pad):
    n, d = x2.shape
    grid = (n // PROJ_TM,)
    row = lambda w: pl.BlockSpec((PROJ_TM, w), lambda i: (i, 0))
    out_shape = ([jax.ShapeDtypeStruct((n, HEAD_W), BF16)] * 3 + [jax.ShapeDtypeStruct((n, HEAD_W), F32)]) * 2 + [
        jax.ShapeDtypeStruct((n, 256), BF16), jax.ShapeDtypeStruct((n, LANES), BF16),
        jax.ShapeDtypeStruct((n, LANES), F32)]
    out_specs = [row(HEAD_W)] * 8 + [row(256), row(LANES), row(LANES)]
    kern = functools.partial(_inproj_kernel, q_scale=A_QK_DIM ** -0.5,
                             idx_scale=(IDX_HEADS ** -0.5) * (IDX_DIM ** -0.5))
    return pl.pallas_call(
        kern, grid=grid,
        in_specs=[row(d), pl.BlockSpec((1, d), lambda i: (0, 0)),
                  pl.BlockSpec(w_pad.shape, lambda i: (0, 0))],
        out_specs=out_specs, out_shape=out_shape,
        compiler_params=pltpu.CompilerParams(dimension_semantics=("arbitrary",),
                                             vmem_limit_bytes=VMEM_LIMIT),
        name="inproj",
    )(x2, norm_w.reshape(1, d), w_pad)


def _bias_kernel(tab_ref, out_ref):
    h = pl.program_id(0)
    t = lax.broadcasted_iota(I32, (KV_TILE, KV_TILE), 0)
    s = lax.broadcasted_iota(I32, (KV_TILE, KV_TILE), 1)
    far = tab_ref[N_BUCKETS // 2 - 1, h]
    for kind in (0, 1):
        rel = s - t - KV_TILE * kind
        n = jnp.abs(rel)
        large = jnp.full_like(n, N_BUCKETS // 4)
        for thr in _T5_THRESHOLDS:
            large = large + (n >= thr).astype(I32)
        bucket = jnp.where(rel > 0, N_BUCKETS // 2, 0) + jnp.where(n < N_BUCKETS // 4, n, large)
        bias = jnp.zeros((KV_TILE, KV_TILE), F32)
        for b in range(N_BUCKETS):
            bias = jnp.where(bucket == b, tab_ref[b, h], bias)
        bias = bias - far
        if kind == 0:
            blocked = jnp.logical_and(h < A_HEADS, (s // CHUNK) > (t // CHUNK))
            bias = jnp.where(blocked, NEG_INF, bias)
        out_ref[kind] = bias


def _bias_tiles(rel_bias):
    nh = rel_bias.shape[1]
    return pl.pallas_call(
        _bias_kernel, grid=(nh,),
        in_specs=[pl.BlockSpec(memory_space=pltpu.SMEM)],
        out_specs=pl.BlockSpec((None, 2, KV_TILE, KV_TILE), lambda h: (h, 0, 0, 0)),
        out_shape=jax.ShapeDtypeStruct((nh, 2, KV_TILE, KV_TILE), F32),
        name="bias_tiles",
    )(rel_bias)


def _stack_masked_queries(q, qm_scr):
    low = lax.broadcasted_iota(I32, (1, LANES), 1) < LANES // 2
    for p in range(N_UNITS // 2):
        qp = q[:, p * LANES:(p + 1) * LANES]
        zero = jnp.zeros_like(qp)
        qm_scr[p, 0:ROWS, 0:LANES] = jnp.where(low, qp, zero)
        qm_scr[p, ROWS:2 * ROWS, 0:LANES] = jnp.where(low, zero, qp)
    if qm_scr.shape[-1] == 2 * LANES:
        hit = (lax.broadcasted_iota(I32, (ROWS, LANES), 0)
               == lax.broadcasted_iota(I32, (ROWS, LANES), 1))
        eye = jnp.where(hit, MASKED_LOGIT, 0.0).astype(BF16)
        for p in range(N_UNITS // 2):
            qm_scr[p, 0:ROWS, LANES:2 * LANES] = eye
            qm_scr[p, ROWS:2 * ROWS, LANES:2 * LANES] = eye


def _two_pass_attention(i, r0, qm_scr, k_ref, v_ref, bias_ref, bias_head, unsel_scr,
                        s_scr, mx_scr, l_scr, acc_scr):
    n_tiles = i + 1
    n_far = jnp.maximum(i - 1, 0)
    mx_scr[...] = jnp.full(mx_scr.shape, NEG_INF, F32)
    l_scr[...] = jnp.zeros(l_scr.shape, F32)
    acc_scr[...] = jnp.zeros(acc_scr.shape, F32)

    def lane_groups(x):
        return [x[:, c * LANES:(c + 1) * LANES] for c in range(x.shape[1] // LANES)]

    def logits_tile(j, width, near):
        start = pl.multiple_of(j * KV_TILE, KV_TILE)
        for p in range(N_UNITS // 2):
            kc = k_ref[pl.ds(start, width), p * LANES:(p + 1) * LANES]
            if unsel_scr is not None:
                kc = jnp.concatenate([kc, unsel_scr[pl.ds(start, width), :]], axis=1)
            s2 = _dot_nt(qm_scr[p], kc)
            for half in range(2):
                u = 2 * p + half
                s = s2[half * ROWS:(half + 1) * ROWS]
                if near:
                    s = s + bias_ref[bias_head(u), i - j, pl.ds(r0, ROWS), :]
                s_scr[u, :, pl.ds(start, width)] = s
                mx_scr[u] = jnp.maximum(mx_scr[u], functools.reduce(jnp.maximum, lane_groups(s)))

    def far_wide(t, carry):
        logits_tile(2 * t, WIDE_TILE, False)
        return carry

    def near_step(j, carry):
        logits_tile(j, KV_TILE, True)
        return carry

    lax.fori_loop(0, n_far // 2, far_wide, 0)

    @pl.when(n_far % 2 == 1)
    def _():
        logits_tile(n_far - 1, KV_TILE, False)

    lax.fori_loop(n_far, n_tiles, near_step, 0)

    for u in range(N_UNITS):
        m = jnp.max(mx_scr[u], axis=1, keepdims=True)
        mx_scr[u] = jnp.broadcast_to(m, (ROWS, LANES))

    def exp_tile(j, width):
        start = pl.multiple_of(j * KV_TILE, KV_TILE)
        for u in range(N_UNITS):
            m = mx_scr[u]
            ps = [jnp.exp(sg - m) for sg in lane_groups(s_scr[u, :, pl.ds(start, width)])]
            l_scr[u] = l_scr[u] + functools.reduce(lambda a, b: a + b, ps)
            pb = jnp.concatenate(ps, axis=1).astype(BF16)
            g = u // 2
            vc = v_ref[pl.ds(start, width), g * LANES:(g + 1) * LANES]
            acc_scr[u] = acc_scr[u] + jnp.dot(pb, vc, preferred_element_type=F32)

    def exp_wide(t, carry):
        exp_tile(2 * t, WIDE_TILE)
        return carry

    lax.fori_loop(0, n_tiles // 2, exp_wide, 0)

    @pl.when(n_tiles % 2 == 1)
    def _():
        exp_tile(n_tiles - 1, KV_TILE)


def _normalised(u, l_scr, acc_scr):
    return acc_scr[u] / jnp.sum(l_scr[u], axis=1, keepdims=True)


def _tree_sum(x):
    parts = [x[t] for t in range(x.shape[0])]
    while len(parts) > 1:
        parts = [parts[t] + parts[t + 1] for t in range(0, len(parts) - 1, 2)] + (
            [parts[-1]] if len(parts) % 2 else [])
    return parts[0]


def _silu(g):
    return g * (1.0 / (1.0 + jnp.exp(-g)))


def _attention_scratch(s, q_width=LANES):
    return [pltpu.VMEM((N_UNITS // 2, 2 * ROWS, q_width), BF16),
            pltpu.VMEM((N_UNITS, ROWS, s), F32),
            pltpu.VMEM((N_UNITS, ROWS, LANES), F32),
            pltpu.VMEM((N_UNITS, ROWS, LANES), F32),
            pltpu.VMEM((N_UNITS, ROWS, LANES), F32)]


def _diff_kernel(lq1, lk1, lq2, lk2, subw_ref, q_ref, k_ref, v_ref, g_ref, bias_ref, o_ref,
                 qm_scr, s_scr, mx_scr, l_scr, acc_scr, *, lambda_init):
    tb = pl.program_id(1)
    i = tb // 2
    r0 = pl.multiple_of((tb % 2) * ROWS, ROWS)
    lam = (jnp.exp(jnp.sum(lq1[...] * lk1[...], axis=-1, keepdims=True))
           - jnp.exp(jnp.sum(lq2[...] * lk2[...], axis=-1, keepdims=True)) + lambda_init)
    _stack_masked_queries(q_ref[...], qm_scr)
    _two_pass_attention(i, r0, qm_scr, k_ref, v_ref, bias_ref, lambda u: u // 2, None,
                        s_scr, mx_scr, l_scr, acc_scr)
    for h in range(A_HEADS):
        o = _normalised(2 * h, l_scr, acc_scr) - lam * _normalised(2 * h + 1, l_scr, acc_scr)
        y = (o * lax.rsqrt(jnp.mean(o * o, axis=-1, keepdims=True) + EPS)) * subw_ref[...]
        y = y * (1.0 - lambda_init)
        cols = slice(h * LANES, (h + 1) * LANES)
        o_ref[:, cols] = (y * _silu(g_ref[:, cols])).astype(BF16)


def _diff_attention(aq, ak, av, ag, bias, lq1, lk1, lq2, lk2, subln_w, lambda_init):
    b, s, _ = aq.shape
    grid = (b, s // ROWS)
    vec = lambda a: pl.BlockSpec((1, a.shape[-1]), lambda bi, t: (0, 0))
    rows = pl.BlockSpec((None, ROWS, HEAD_W), lambda bi, t: (bi, t, 0))
    full = _resident((None, s, HEAD_W), lambda bi, t: (bi, 0, 0))
    kern = functools.partial(_diff_kernel, lambda_init=lambda_init)
    return pl.pallas_call(
        kern, grid=grid,
        in_specs=[vec(lq1), vec(lk1), vec(lq2), vec(lk2), vec(subln_w), rows, full, full, rows,
                  _resident(bias.shape, lambda bi, t: (0, 0, 0, 0))],
        out_specs=rows,
        out_shape=jax.ShapeDtypeStruct((b, s, HEAD_W), BF16),
        scratch_shapes=_attention_scratch(s),
        compiler_params=pltpu.CompilerParams(dimension_semantics=("arbitrary",) * 2,
                                             vmem_limit_bytes=VMEM_LIMIT),
        name="diff_attention",
    )(lq1, lk1, lq2, lk2, subln_w, aq, ak, av, ag, bias)


def _dsa_kernel(iq_ref, ik_ref, iw_ref, q_ref, k_ref, v_ref, g_ref, bias_ref, o_ref,
                sc_scr, iqm_scr, unsel_scr, qm_scr, s_scr, mx_scr, l_scr, acc_scr, *, k_sel):
    tb = pl.program_id(1)
    i = tb // 2
    n_tiles = i + 1
    n_scan = (n_tiles + 1) // 2
    r0 = pl.multiple_of((tb % 2) * ROWS, ROWS)
    lane = lax.broadcasted_iota(I32, (1, LANES), 1)
    low = lane < B_HEAD_DIM
    q_chunk = (tb * ROWS + lane) // CHUNK
    key_in_tile = lax.broadcasted_iota(I32, (KV_TILE, LANES), 0)

    iq = iq_ref[...]
    for hh in range(IDX_HEADS):
        pair = iq[:, (hh // 2) * LANES:(hh // 2 + 1) * LANES]
        iqm_scr[hh * ROWS:(hh + 1) * ROWS, :] = jnp.where(low == (hh % 2 == 0), pair,
                                                          jnp.zeros_like(pair))
    iw_t = iw_ref[...].T

    def score_tile(j, diagonal):
        start = pl.multiple_of(j * KV_TILE, KV_TILE)
        logits = _dot_nt(ik_ref[pl.ds(start, KV_TILE), :], iqm_scr[...])
        sc = jnp.zeros((KV_TILE, LANES), F32)
        for hh in range(IDX_HEADS):
            sc = sc + iw_t[hh:hh + 1, :] * jnp.maximum(logits[:, hh * ROWS:(hh + 1) * ROWS], 0.0)
        if diagonal:
            sc = jnp.where(((start + key_in_tile) // CHUNK) <= q_chunk, sc, NEG_INF)
        sc_scr[pl.ds(start, KV_TILE), :] = sc

    def far_scores(t, carry):
        score_tile(2 * t, False)
        score_tile(2 * t + 1, False)
        return carry

    lax.fori_loop(0, i // 2, far_scores, 0)

    @pl.when(i % 2 == 1)
    def _():
        score_tile(i - 1, False)

    score_tile(i, True)

    @pl.when(n_tiles % 2 == 1)
    def _():
        pad = pl.ds(pl.multiple_of(n_tiles * KV_TILE, KV_TILE), KV_TILE)
        sc_scr[pad, :] = jnp.full((KV_TILE, LANES), NEG_INF, F32)

    def key_to_float(key):
        bits = key ^ (lax.shift_right_arithmetic(key, 31) & 0x7FFFFFFF)
        return lax.bitcast_convert_type(bits, F32)

    def scan_count(pred):
        def body(t, acc):
            blk = sc_scr[pl.ds(pl.multiple_of(t * SCAN_TILE, SCAN_TILE), SCAN_TILE), :]
            hit = jnp.where(pred(blk.reshape(SCAN_TILE // 8, 8, LANES)), 1.0, 0.0)
            return acc + _tree_sum(hit)
        acc = lax.fori_loop(0, n_scan, body, jnp.zeros((8, LANES), F32))
        return jnp.sum(acc, axis=0, keepdims=True)

    kf = float(k_sel)

    def bit_step(b, carry):
        key, c_key = carry
        cand = key + lax.shift_left(jnp.int32(1), 31 - b)
        cand_f = key_to_float(cand)
        c = scan_count(lambda blk: blk >= cand_f)
        ok = jnp.logical_or(c >= kf, cand <= KEY_NEG_INF)
        return jnp.where(ok, cand, key), jnp.where(ok, c, c_key)

    n_keys = jnp.full((1, LANES), 1.0, F32) * (n_tiles * KV_TILE).astype(F32)
    thr_key, c_thr = lax.fori_loop(0, 32, bit_step,
                                   (jnp.full((1, LANES), INT_MIN, I32), n_keys))
    thr = key_to_float(thr_key)

    def tile_count(pred):
        def body(j, acc):
            start = pl.multiple_of(j * KV_TILE, KV_TILE)
            hit = jnp.where(pred(sc_scr[pl.ds(start, KV_TILE), :], start + key_in_tile), 1.0, 0.0)
            return acc + _tree_sum(hit.reshape(KV_TILE // 8, 8, LANES))
        acc = lax.fori_loop(0, n_tiles, body, jnp.zeros((8, LANES), F32))
        return jnp.sum(acc, axis=0, keepdims=True)

    def tie_limit():
        need = kf - tile_count(lambda kc, idx: kc > thr)

        def idx_step(b, lim):
            cand = lim + lax.shift_left(jnp.int32(1), 11 - b)
            f = tile_count(lambda kc, idx: jnp.logical_and(kc == thr, idx < cand))
            return jnp.where(f < need, cand, lim)

        return lax.fori_loop(0, 12, idx_step, jnp.zeros((1, LANES), I32))

    lim = lax.cond(jnp.max(c_thr) > kf, tie_limit,
                   lambda: jnp.full((1, LANES), 2 ** 30, I32))

    def mask_tile(j, diagonal):
        start = pl.multiple_of(j * KV_TILE, KV_TILE)
        kc = sc_scr[pl.ds(start, KV_TILE), :]
        idx = start + key_in_tile
        sel = jnp.logical_or(kc > thr, jnp.logical_and(kc == thr, idx <= lim))
        if diagonal:
            sel = jnp.logical_and(sel, (idx // CHUNK) <= q_chunk)
        unsel_scr[pl.ds(start, KV_TILE), :] = jnp.where(sel, 0.0, 1.0).astype(BF16)

    def far_mask(j, carry):
        mask_tile(j, False)
        return carry

    lax.fori_loop(0, i, far_mask, 0)
    mask_tile(i, True)

    _stack_masked_queries(q_ref[...], qm_scr)
    _two_pass_attention(i, r0, qm_scr, k_ref, v_ref, bias_ref, lambda u: u, unsel_scr,
                        s_scr, mx_scr, l_scr, acc_scr)
    for p in range(B_HEADS // 2):
        o = jnp.where(low, _normalised(2 * p, l_scr, acc_scr), _normalised(2 * p + 1, l_scr, acc_scr))
        cols = slice(p * LANES, (p + 1) * LANES)
        o_ref[:, cols] = (o * _silu(g_ref[:, cols])).astype(BF16)


def _dsa_attention(iq, ik, iw, bq, bk, bv, bg, bias, k_sel):
    b, s, _ = bq.shape
    grid = (b, s // ROWS)
    rows = lambda w: pl.BlockSpec((None, ROWS, w), lambda bi, t: (bi, t, 0))
    full = lambda w: _resident((None, s, w), lambda bi, t: (bi, 0, 0))
    kern = functools.partial(_dsa_kernel, k_sel=k_sel)
    return pl.pallas_call(
        kern, grid=grid,
        in_specs=[rows(256), full(LANES), rows(LANES), rows(HEAD_W), full(HEAD_W), full(HEAD_W),
                  rows(HEAD_W), _resident(bias.shape, lambda bi, t: (0, 0, 0, 0))],
        out_specs=rows(HEAD_W),
        out_shape=jax.ShapeDtypeStruct((b, s, HEAD_W), BF16),
        scratch_shapes=[pltpu.VMEM((s, LANES), F32),
                        pltpu.VMEM((IDX_HEADS * ROWS, LANES), BF16),
                        pltpu.VMEM((s, LANES), BF16)]
        + _attention_scratch(s, 2 * LANES),
        compiler_params=pltpu.CompilerParams(dimension_semantics=("arbitrary",) * 2,
                                             vmem_limit_bytes=VMEM_LIMIT),
        name="dsa_attention",
    )(iq, ik, iw, bq, bk, bv, bg, bias)


def _outproj_kernel(ao_ref, bo_ref, x_ref, w_ref, fw_ref, o_ref):
    y = jnp.dot(ao_ref[...], w_ref[0:HEAD_W, :], preferred_element_type=F32)
    y = y + jnp.dot(bo_ref[...], w_ref[HEAD_W:2 * HEAD_W, :], preferred_element_type=F32)
    z = x_ref[...] + y
    o_ref[...] = (z * lax.rsqrt(jnp.mean(z * z, axis=-1, keepdims=True) + EPS)) * fw_ref[...]


def _outproj(ao, bo, x2, w_out, final_w):
    n, d = x2.shape
    row = lambda w: pl.BlockSpec((PROJ_TM, w), lambda i: (i, 0))
    return pl.pallas_call(
        _outproj_kernel, grid=(n // PROJ_TM,),
        in_specs=[row(HEAD_W), row(HEAD_W), row(d), pl.BlockSpec(w_out.shape, lambda i: (0, 0)),
                  pl.BlockSpec((1, d), lambda i: (0, 0))],
        out_specs=row(d), out_shape=jax.ShapeDtypeStruct((n, d), F32),
        compiler_params=pltpu.CompilerParams(dimension_semantics=("arbitrary",),
                                             vmem_limit_bytes=VMEM_LIMIT),
        name="outproj",
    )(ao, bo, x2, w_out, final_w.reshape(1, d))


def kernel(x, norm_w, w_in, w_out, lambda_q1, lambda_k1, lambda_q2, lambda_k2, subln_w, rel_bias,
           final_norm_w):
    b, s, d = x.shape
    depth = norm_w.shape[0]
    assert depth == 1, "single-layer trunk"
    assert s % KV_TILE == 0 and (b * s) % PROJ_TM == 0
    layer = 0
    lambda_init = 0.8 - 0.6 * math.exp(-0.3 * layer)
    k_sel = min(TOPK_MAX, s // 4)

    w = w_in[layer]
    main = 8 * HEAD_W
    iq_w = w[:, main:main + IDX_HEADS * IDX_DIM]
    ik_w = w[:, main + IDX_HEADS * IDX_DIM:main + IDX_HEADS * IDX_DIM + IDX_DIM]
    iw_w = w[:, main + IDX_HEADS * IDX_DIM + IDX_DIM:]
    pad = jnp.zeros((d, LANES - IDX_HEADS), w.dtype)
    w_pad = jnp.concatenate([w[:, :main], iq_w, ik_w, ik_w, iw_w, pad], axis=1).astype(BF16)

    x2 = x.reshape(b * s, d)
    aq, ak, av, ag, bq, bk, bv, bg, iq, ik, iw = _inproj(x2, norm_w[layer], w_pad)
    r3 = lambda a: a.reshape(b, s, a.shape[-1])
    bias = _bias_tiles(rel_bias)

    vec = lambda a: a.reshape(1, -1)
    ao = _diff_attention(r3(aq), r3(ak), r3(av), r3(ag), bias[:A_HEADS],
                         vec(lambda_q1[layer]), vec(lambda_k1[layer]),
                         vec(lambda_q2[layer]), vec(lambda_k2[layer]), vec(subln_w[layer]),
                         lambda_init)
    bo = _dsa_attention(r3(iq), r3(ik), r3(iw), r3(bq), r3(bk), r3(bv), r3(bg),
                        bias[A_HEADS:], k_sel)

    out = _outproj(ao.reshape(b * s, -1), bo.reshape(b * s, -1), x2,
                   w_out[layer].astype(BF16), final_norm_w)
    return out.reshape(b, s, d)
```

```python
import functools
import math

import jax
import jax.numpy as jnp
from jax import lax
from jax.experimental import pallas as pl
from jax.experimental.pallas import tpu as pltpu

F32 = jnp.float32
BF16 = jnp.bfloat16
I32 = jnp.int32

EPS = 1e-6
CHUNK = 64
A_HEADS = 4
A_QK_DIM = 64
A_V_DIM = 128
B_HEADS = 8
B_HEAD_DIM = 64
IDX_HEADS = 4
IDX_DIM = 64
TOPK_MAX = 256
N_BUCKETS = 32
HEAD_W = 512
LANES = 128
ROWS = 128
KV_TILE = 256
N_UNITS = 8
PROJ_TM = 512
SCAN_TILE = 512
INT_MIN = -2 ** 31
KEY_NEG_INF = INT_MIN + 0x7FFFFF
NEG_INF = float("-inf")
MASKED_LOGIT = -(2.0 ** 120)
VMEM_LIMIT = 56 * 1024 * 1024

_T5_THRESHOLDS = (12, 16, 23, 32, 46, 64, 91)


def _dot_nt(a, b):
    return lax.dot_general(a, b, (((1,), (1,)), ((), ())), preferred_element_type=F32)


def _resident(shape, index_map):
    return pl.BlockSpec(shape, index_map, pipeline_mode=pl.Buffered(1))


def _inproj_kernel(x_ref, nw_ref, w_ref, aq, ak, av, ag, bq, bk, bv, bg, iq, ik, iw,
                   *, q_scale, idx_scale):
    x = x_ref[...]
    ms = jnp.mean(x * x, axis=-1, keepdims=True)
    h = ((x * lax.rsqrt(ms + EPS)) * nw_ref[...]).astype(BF16)

    def proj(c):
        return jnp.dot(h, w_ref[:, c * HEAD_W:(c + 1) * HEAD_W], preferred_element_type=F32)

    aq[...] = (proj(0) * q_scale).astype(BF16)
    ak[...] = proj(1).astype(BF16)
    av[...] = proj(2).astype(BF16)
    ag[...] = proj(3)
    bq[...] = (proj(4) * q_scale).astype(BF16)
    bk[...] = proj(5).astype(BF16)
    bv[...] = proj(6).astype(BF16)
    bg[...] = proj(7)
    tail = proj(8)
    iq[...] = tail[:, 0:256].astype(BF16)
    ik[...] = tail[:, 256:384].astype(BF16)
    iw[...] = tail[:, 384:512] * idx_scale


def _inproj(x2, norm_w, w_pad):
    n, d = x2.shape
    grid = (n // PROJ_TM,)
    row = lambda w: pl.BlockSpec((PROJ_TM, w), lambda i: (i, 0))
    out_shape = ([jax.ShapeDtypeStruct((n, HEAD_W), BF16)] * 3 + [jax.ShapeDtypeStruct((n, HEAD_W), F32)]) * 2 + [
        jax.ShapeDtypeStruct((n, 256), BF16), jax.ShapeDtypeStruct((n, LANES), BF16),
        jax.ShapeDtypeStruct((n, LANES), F32)]
    out_specs = [row(HEAD_W)] * 8 + [row(256), row(LANES), row(LANES)]
    kern = functools.partial(_inproj_kernel, q_scale=A_QK_DIM ** -0.5,
                             idx_scale=(IDX_HEADS ** -0.5) * (IDX_DIM ** -0.5))
    return pl.pallas_call(
        kern, grid=grid,
        in_specs=[row(d), pl.BlockSpec((1, d), lambda i: (0, 0)),
                  pl.BlockSpec(w_pad.shape, lambda i: (0, 0))],
        out_specs=out_specs, out_shape=out_shape,
        compiler_params=pltpu.CompilerParams(dimension_semantics=("arbitrary",),
                                             vmem_limit_bytes=VMEM_LIMIT),
        name="inproj",
    )(x2, norm_w.reshape(1, d), w_pad)


def _bias_kernel(tab_ref, out_ref):
    h = pl.program_id(0)
    t = lax.broadcasted_iota(I32, (KV_TILE, KV_TILE), 0)
    s = lax.broadcasted_iota(I32, (KV_TILE, KV_TILE), 1)
    far = tab_ref[N_BUCKETS // 2 - 1, h]
    for kind in (0, 1):
        rel = s - t - KV_TILE * kind
        n = jnp.abs(rel)
        large = jnp.full_like(n, N_BUCKETS // 4)
        for thr in _T5_THRESHOLDS:
            large = large + (n >= thr).astype(I32)
        bucket = jnp.where(rel > 0, N_BUCKETS // 2, 0) + jnp.where(n < N_BUCKETS // 4, n, large)
        bias = jnp.zeros((KV_TILE, KV_TILE), F32)
        for b in range(N_BUCKETS):
            bias = jnp.where(bucket == b, tab_ref[b, h], bias)
        bias = bias - far
        if kind == 0:
            blocked = jnp.logical_and(h < A_HEADS, (s // CHUNK) > (t // CHUNK))
            bias = jnp.where(blocked, NEG_INF, bias)
        out_ref[kind] = bias


def _bias_tiles(rel_bias):
    nh = rel_bias.shape[1]
    return pl.pallas_call(
        _bias_kernel, grid=(nh,),
        in_specs=[pl.BlockSpec(memory_space=pltpu.SMEM)],
        out_specs=pl.BlockSpec((None, 2, KV_TILE, KV_TILE), lambda h: (h, 0, 0, 0)),
        out_shape=jax.ShapeDtypeStruct((nh, 2, KV_TILE, KV_TILE), F32),
        name="bias_tiles",
    )(rel_bias)


def _stack_masked_queries(q, qm_scr):
    low = lax.broadcasted_iota(I32, (1, LANES), 1) < LANES // 2
    for p in range(N_UNITS // 2):
        qp = q[:, p * LANES:(p + 1) * LANES]
        zero = jnp.zeros_like(qp)
        qm_scr[p, 0:ROWS, 0:LANES] = jnp.where(low, qp, zero)
        qm_scr[p, ROWS:2 * ROWS, 0:LANES] = jnp.where(low, zero, qp)
    if qm_scr.shape[-1] == 2 * LANES:
        hit = (lax.broadcasted_iota(I32, (ROWS, LANES), 0)
               == lax.broadcasted_iota(I32, (ROWS, LANES), 1))
        eye = jnp.where(hit, MASKED_LOGIT, 0.0).astype(BF16)
        for p in range(N_UNITS // 2):
            qm_scr[p, 0:ROWS, LANES:2 * LANES] = eye
            qm_scr[p, ROWS:2 * ROWS, LANES:2 * LANES] = eye


def _two_pass_attention(i, r0, qm_scr, k_ref, v_ref, bias_ref, bias_head, unsel_scr,
                        s_scr, mx_scr, l_scr, acc_scr):
    n_tiles = i + 1
    n_far = jnp.maximum(i - 1, 0)
    mx_scr[...] = jnp.full(mx_scr.shape, NEG_INF, F32)
    l_scr[...] = jnp.zeros(l_scr.shape, F32)
    acc_scr[...] = jnp.zeros(acc_scr.shape, F32)

    def lane_groups(x):
        return [x[:, c * LANES:(c + 1) * LANES] for c in range(x.shape[1] // LANES)]

    def run_tiles(count, fn):
        def body(t, carry):
            fn(4 * t, 4 * KV_TILE)
            return carry
        lax.fori_loop(0, count // 4, body, 0)

        @pl.when(count % 4 >= 2)
        def _():
            fn(4 * (count // 4), 2 * KV_TILE)

        @pl.when(count % 2 == 1)
        def _():
            fn(count - 1, KV_TILE)

    def logits_tile(j, width, bias_kinds=()):
        start = pl.multiple_of(j * KV_TILE, KV_TILE)
        for p in range(N_UNITS // 2):
            kc = k_ref[pl.ds(start, width), p * LANES:(p + 1) * LANES]
            if unsel_scr is not None:
                kc = jnp.concatenate([kc, unsel_scr[pl.ds(start, width), :]], axis=1)
            s2 = _dot_nt(qm_scr[p], kc)
            for half in range(2):
                u = 2 * p + half
                s = s2[half * ROWS:(half + 1) * ROWS]
                if bias_kinds:
                    s = s + jnp.concatenate(
                        [bias_ref[bias_head(u), kind, pl.ds(r0, ROWS), :] for kind in bias_kinds],
                        axis=1)
                s_scr[u, :, pl.ds(start, width)] = s
                mx_scr[u] = jnp.maximum(mx_scr[u], functools.reduce(jnp.maximum, lane_groups(s)))

    run_tiles(n_far, logits_tile)

    @pl.when(i >= 1)
    def _():
        logits_tile(i - 1, 2 * KV_TILE, (1, 0))

    @pl.when(i == 0)
    def _():
        logits_tile(0, KV_TILE, (0,))

    for u in range(N_UNITS):
        m = jnp.max(mx_scr[u], axis=1, keepdims=True)
        mx_scr[u] = jnp.broadcast_to(m, (ROWS, LANES))

    def exp_tile(j, width):
        start = pl.multiple_of(j * KV_TILE, KV_TILE)
        for u in range(N_UNITS):
            m = mx_scr[u]
            ps = [jnp.exp(sg - m) for sg in lane_groups(s_scr[u, :, pl.ds(start, width)])]
            l_scr[u] = l_scr[u] + functools.reduce(lambda a, b: a + b, ps)
            pb = jnp.concatenate(ps, axis=1).astype(BF16)
            g = u // 2
            vc = v_ref[pl.ds(start, width), g * LANES:(g + 1) * LANES]
            acc_scr[u] = acc_scr[u] + jnp.dot(pb, vc, preferred_element_type=F32)

    run_tiles(n_tiles, exp_tile)


def _normalised(u, l_scr, acc_scr):
    return acc_scr[u] / jnp.sum(l_scr[u], axis=1, keepdims=True)


def _tree_sum(x):
    parts = [x[t] for t in range(x.shape[0])]
    while len(parts) > 1:
        parts = [parts[t] + parts[t + 1] for t in range(0, len(parts) - 1, 2)] + (
            [parts[-1]] if len(parts) % 2 else [])
    return parts[0]


def _silu(g):
    return g * (1.0 / (1.0 + jnp.exp(-g)))


def _attention_scratch(s, q_width=LANES):
    return [pltpu.VMEM((N_UNITS // 2, 2 * ROWS, q_width), BF16),
            pltpu.VMEM((N_UNITS, ROWS, s), F32),
            pltpu.VMEM((N_UNITS, ROWS, LANES), F32),
            pltpu.VMEM((N_UNITS, ROWS, LANES), F32),
            pltpu.VMEM((N_UNITS, ROWS, LANES), F32)]


def _diff_kernel(lq1, lk1, lq2, lk2, subw_ref, q_ref, k_ref, v_ref, g_ref, bias_ref, o_ref,
                 qm_scr, s_scr, mx_scr, l_scr, acc_scr, *, lambda_init):
    tb = pl.program_id(1)
    i = tb // 2
    r0 = pl.multiple_of((tb % 2) * ROWS, ROWS)
    lam = (jnp.exp(jnp.sum(lq1[...] * lk1[...], axis=-1, keepdims=True))
           - jnp.exp(jnp.sum(lq2[...] * lk2[...], axis=-1, keepdims=True)) + lambda_init)
    _stack_masked_queries(q_ref[...], qm_scr)
    _two_pass_attention(i, r0, qm_scr, k_ref, v_ref, bias_ref, lambda u: u // 2, None,
                        s_scr, mx_scr, l_scr, acc_scr)
    for h in range(A_HEADS):
        o = _normalised(2 * h, l_scr, acc_scr) - lam * _normalised(2 * h + 1, l_scr, acc_scr)
        y = (o * lax.rsqrt(jnp.mean(o * o, axis=-1, keepdims=True) + EPS)) * subw_ref[...]
        y = y * (1.0 - lambda_init)
        cols = slice(h * LANES, (h + 1) * LANES)
        o_ref[:, cols] = (y * _silu(g_ref[:, cols])).astype(BF16)


def _diff_attention(aq, ak, av, ag, bias, lq1, lk1, lq2, lk2, subln_w, lambda_init):
    b, s, _ = aq.shape
    grid = (b, s // ROWS)
    vec = lambda a: pl.BlockSpec((1, a.shape[-1]), lambda bi, t: (0, 0))
    rows = pl.BlockSpec((None, ROWS, HEAD_W), lambda bi, t: (bi, t, 0))
    full = _resident((None, s, HEAD_W), lambda bi, t: (bi, 0, 0))
    kern = functools.partial(_diff_kernel, lambda_init=lambda_init)
    return pl.pallas_call(
        kern, grid=grid,
        in_specs=[vec(lq1), vec(lk1), vec(lq2), vec(lk2), vec(subln_w), rows, full, full, rows,
                  _resident(bias.shape, lambda bi, t: (0, 0, 0, 0))],
        out_specs=rows,
        out_shape=jax.ShapeDtypeStruct((b, s, HEAD_W), BF16),
        scratch_shapes=_attention_scratch(s),
        compiler_params=pltpu.CompilerParams(dimension_semantics=("arbitrary",) * 2,
                                             vmem_limit_bytes=VMEM_LIMIT),
        name="diff_attention",
    )(lq1, lk1, lq2, lk2, subln_w, aq, ak, av, ag, bias)


def _dsa_kernel(iq_ref, ik_ref, iw_ref, q_ref, k_ref, v_ref, g_ref, bias_ref, o_ref,
                sc_scr, iqm_scr, unsel_scr, qm_scr, s_scr, mx_scr, l_scr, acc_scr, *, k_sel):
    tb = pl.program_id(1)
    i = tb // 2
    n_tiles = i + 1
    n_scan = (n_tiles + 1) // 2
    r0 = pl.multiple_of((tb % 2) * ROWS, ROWS)
    lane = lax.broadcasted_iota(I32, (1, LANES), 1)
    low = lane < B_HEAD_DIM
    q_chunk = (tb * ROWS + lane) // CHUNK
    key_in_tile = lax.broadcasted_iota(I32, (KV_TILE, LANES), 0)

    iq = iq_ref[...]
    for hh in range(IDX_HEADS):
        pair = iq[:, (hh // 2) * LANES:(hh // 2 + 1) * LANES]
        iqm_scr[hh * ROWS:(hh + 1) * ROWS, :] = jnp.where(low == (hh % 2 == 0), pair,
                                                          jnp.zeros_like(pair))
    iw_t = iw_ref[...].T

    def score_tile(j, diagonal):
        start = pl.multiple_of(j * KV_TILE, KV_TILE)
        logits = _dot_nt(ik_ref[pl.ds(start, KV_TILE), :], iqm_scr[...])
        sc = jnp.zeros((KV_TILE, LANES), F32)
        for hh in range(IDX_HEADS):
            sc = sc + iw_t[hh:hh + 1, :] * jnp.maximum(logits[:, hh * ROWS:(hh + 1) * ROWS], 0.0)
        if diagonal:
            sc = jnp.where(((start + key_in_tile) // CHUNK) <= q_chunk, sc, NEG_INF)
        sc_scr[pl.ds(start, KV_TILE), :] = sc

    def far_scores(t, carry):
        score_tile(2 * t, False)
        score_tile(2 * t + 1, False)
        return carry

    lax.fori_loop(0, i // 2, far_scores, 0)

    @pl.when(i % 2 == 1)
    def _():
        score_tile(i - 1, False)

    score_tile(i, True)

    @pl.when(n_tiles % 2 == 1)
    def _():
        pad = pl.ds(pl.multiple_of(n_tiles * KV_TILE, KV_TILE), KV_TILE)
        sc_scr[pad, :] = jnp.full((KV_TILE, LANES), NEG_INF, F32)

    def key_to_float(key):
        bits = key ^ (lax.shift_right_arithmetic(key, 31) & 0x7FFFFFFF)
        return lax.bitcast_convert_type(bits, F32)

    def scan_count(pred):
        def body(t, acc):
            blk = sc_scr[pl.ds(pl.multiple_of(t * SCAN_TILE, SCAN_TILE), SCAN_TILE), :]
            hit = jnp.where(pred(blk.reshape(SCAN_TILE // 8, 8, LANES)), 1.0, 0.0)
            return acc + _tree_sum(hit)
        acc = lax.fori_loop(0, n_scan, body, jnp.zeros((8, LANES), F32))
        return jnp.sum(acc, axis=0, keepdims=True)

    kf = float(k_sel)

    def bit_step(b, carry):
        key, c_key = carry
        cand = key + lax.shift_left(jnp.int32(1), 31 - b)
        cand_f = key_to_float(cand)
        c = scan_count(lambda blk: blk >= cand_f)
        ok = jnp.logical_or(c >= kf, cand <= KEY_NEG_INF)
        return jnp.where(ok, cand, key), jnp.where(ok, c, c_key)

    n_keys = jnp.full((1, LANES), 1.0, F32) * (n_tiles * KV_TILE).astype(F32)
    thr_key, c_thr = lax.fori_loop(0, 32, bit_step,
                                   (jnp.full((1, LANES), INT_MIN, I32), n_keys))
    thr = key_to_float(thr_key)

    def tile_count(pred):
        def body(j, acc):
            start = pl.multiple_of(j * KV_TILE, KV_TILE)
            hit = jnp.where(pred(sc_scr[pl.ds(start, KV_TILE), :], start + key_in_tile), 1.0, 0.0)
            return acc + _tree_sum(hit.reshape(KV_TILE // 8, 8, LANES))
        acc = lax.fori_loop(0, n_tiles, body, jnp.zeros((8, LANES), F32))
        return jnp.sum(acc, axis=0, keepdims=True)

    def tie_limit():
        need = kf - tile_count(lambda kc, idx: kc > thr)

        def idx_step(b, lim):
            cand = lim + lax.shift_left(jnp.int32(1), 11 - b)
            f = tile_count(lambda kc, idx: jnp.logical_and(kc == thr, idx < cand))
            return jnp.where(f < need, cand, lim)

        return lax.fori_loop(0, 12, idx_step, jnp.zeros((1, LANES), I32))

    lim = lax.cond(jnp.max(c_thr) > kf, tie_limit,
                   lambda: jnp.full((1, LANES), 2 ** 30, I32))

    def mask_tile(j, diagonal):
        start = pl.multiple_of(j * KV_TILE, KV_TILE)
        kc = sc_scr[pl.ds(start, KV_TILE), :]
        idx = start + key_in_tile
        sel = jnp.logical_or(kc > thr, jnp.logical_and(kc == thr, idx <= lim))
        if diagonal:
            sel = jnp.logical_and(sel, (idx // CHUNK) <= q_chunk)
        unsel_scr[pl.ds(start, KV_TILE), :] = jnp.where(sel, 0.0, 1.0).astype(BF16)

    def far_mask(j, carry):
        mask_tile(j, False)
        return carry

    lax.fori_loop(0, i, far_mask, 0)
    mask_tile(i, True)

    _stack_masked_queries(q_ref[...], qm_scr)
    _two_pass_attention(i, r0, qm_scr, k_ref, v_ref, bias_ref, lambda u: u, unsel_scr,
                        s_scr, mx_scr, l_scr, acc_scr)
    for p in range(B_HEADS // 2):
        o = jnp.where(low, _normalised(2 * p, l_scr, acc_scr), _normalised(2 * p + 1, l_scr, acc_scr))
        cols = slice(p * LANES, (p + 1) * LANES)
        o_ref[:, cols] = (o * _silu(g_ref[:, cols])).astype(BF16)


def _dsa_attention(iq, ik, iw, bq, bk, bv, bg, bias, k_sel):
    b, s, _ = bq.shape
    grid = (b, s // ROWS)
    rows = lambda w: pl.BlockSpec((None, ROWS, w), lambda bi, t: (bi, t, 0))
    full = lambda w: _resident((None, s, w), lambda bi, t: (bi, 0, 0))
    kern = functools.partial(_dsa_kernel, k_sel=k_sel)
    return pl.pallas_call(
        kern, grid=grid,
        in_specs=[rows(256), full(LANES), rows(LANES), rows(HEAD_W), full(HEAD_W), full(HEAD_W),
                  rows(HEAD_W), _resident(bias.shape, lambda bi, t: (0, 0, 0, 0))],
        out_specs=rows(HEAD_W),
        out_shape=jax.ShapeDtypeStruct((b, s, HEAD_W), BF16),
        scratch_shapes=[pltpu.VMEM((s, LANES), F32),
                        pltpu.VMEM((IDX_HEADS * ROWS, LANES), BF16),
                        pltpu.VMEM((s, LANES), BF16)]
        + _attention_scratch(s, 2 * LANES),
        compiler_params=pltpu.CompilerParams(dimension_semantics=("arbitrary",) * 2,
                                             vmem_limit_bytes=VMEM_LIMIT),
        name="dsa_attention",
    )(iq, ik, iw, bq, bk, bv, bg, bias)


def _outproj_kernel(ao_ref, bo_ref, x_ref, w_ref, fw_ref, o_ref):
    y = jnp.dot(ao_ref[...], w_ref[0:HEAD_W, :], preferred_element_type=F32)
    y = y + jnp.dot(bo_ref[...], w_ref[HEAD_W:2 * HEAD_W, :], preferred_element_type=F32)
    z = x_ref[...] + y
    o_ref[...] = (z * lax.rsqrt(jnp.mean(z * z, axis=-1, keepdims=True) + EPS)) * fw_ref[...]


def _outproj(ao, bo, x2, w_out, final_w):
    n, d = x2.shape
    row = lambda w: pl.BlockSpec((PROJ_TM, w), lambda i: (i, 0))
    return pl.pallas_call(
        _outproj_kernel, grid=(n // PROJ_TM,),
        in_specs=[row(HEAD_W), row(HEAD_W), row(d), pl.BlockSpec(w_out.shape, lambda i: (0, 0)),
                  pl.BlockSpec((1, d), lambda i: (0, 0))],
        out_specs=row(d), out_shape=jax.ShapeDtypeStruct((n, d), F32),
        compiler_params=pltpu.CompilerParams(dimension_semantics=("arbitrary",),
                                             vmem_limit_bytes=VMEM_LIMIT),
        name="outproj",
    )(ao, bo, x2, w_out, final_w.reshape(1, d))


def kernel(x, norm_w, w_in, w_out, lambda_q1, lambda_k1, lambda_q2, lambda_k2, subln_w, rel_bias,
           final_norm_w):
    b, s, d = x.shape
    depth = norm_w.shape[0]
    assert depth == 1, "single-layer trunk"
    assert s % KV_TILE == 0 and (b * s) % PROJ_TM == 0
    layer = 0
    lambda_init = 0.8 - 0.6 * math.exp(-0.3 * layer)
    k_sel = min(TOPK_MAX, s // 4)

    w = w_in[layer]
    main = 8 * HEAD_W
    iq_w = w[:, main:main + IDX_HEADS * IDX_DIM]
    ik_w = w[:, main + IDX_HEADS * IDX_DIM:main + IDX_HEADS * IDX_DIM + IDX_DIM]
    iw_w = w[:, main + IDX_HEADS * IDX_DIM + IDX_DIM:]
    pad = jnp.zeros((d, LANES - IDX_HEADS), w.dtype)
    w_pad = jnp.concatenate([w[:, :main], iq_w, ik_w, ik_w, iw_w, pad], axis=1).astype(BF16)

    x2 = x.reshape(b * s, d)
    aq, ak, av, ag, bq, bk, bv, bg, iq, ik, iw = _inproj(x2, norm_w[layer], w_pad)
    r3 = lambda a: a.reshape(b, s, a.shape[-1])
    bias = _bias_tiles(rel_bias)

    vec = lambda a: a.reshape(1, -1)
    ao = _diff_attention(r3(aq), r3(ak), r3(av), r3(ag), bias[:A_HEADS],
                         vec(lambda_q1[layer]), vec(lambda_k1[layer]),
                         vec(lambda_q2[layer]), vec(lambda_k2[layer]), vec(subln_w[layer]),
                         lambda_init)
    bo = _dsa_attention(r3(iq), r3(ik), r3(iw), r3(bq), r3(bk), r3(bv), r3(bg),
                        bias[A_HEADS:], k_sel)

    out = _outproj(ao.reshape(b * s, -1), bo.reshape(b * s, -1), x2,
                   w_out[layer].astype(BF16), final_norm_w)
    return out.reshape(b, s, d)
```

```python
import functools
import math

import jax
import jax.numpy as jnp
from jax import lax
from jax.experimental import pallas as pl
from jax.experimental.pallas import tpu as pltpu

F32 = jnp.float32
BF16 = jnp.bfloat16
I32 = jnp.int32

EPS = 1e-6
CHUNK = 64
A_HEADS = 4
A_QK_DIM = 64
A_V_DIM = 128
B_HEADS = 8
B_HEAD_DIM = 64
IDX_HEADS = 4
IDX_DIM = 64
TOPK_MAX = 256
N_BUCKETS = 32
HEAD_W = 512
LANES = 128
ROWS = 128
KV_TILE = 256
N_UNITS = 8
PROJ_TM = 512
SCAN_TILE = 512
SCAN_CHAINS = 8
INT_MIN = -2 ** 31
KEY_NEG_INF = INT_MIN + 0x7FFFFF
NEG_INF = float("-inf")
MASKED_LOGIT = -(2.0 ** 120)
VMEM_LIMIT = 56 * 1024 * 1024

_T5_THRESHOLDS = (12, 16, 23, 32, 46, 64, 91)


def _dot_nt(a, b):
    return lax.dot_general(a, b, (((1,), (1,)), ((), ())), preferred_element_type=F32)


def _resident(shape, index_map):
    return pl.BlockSpec(shape, index_map, pipeline_mode=pl.Buffered(1))


def _inproj_kernel(x_ref, nw_ref, w_ref, aq, ak, av, ag, bq, bk, bv, bg, iq, ik, iw,
                   *, q_scale, idx_scale):
    x = x_ref[...]
    ms = jnp.mean(x * x, axis=-1, keepdims=True)
    h = ((x * lax.rsqrt(ms + EPS)) * nw_ref[...]).astype(BF16)

    def proj(c):
        return jnp.dot(h, w_ref[:, c * HEAD_W:(c + 1) * HEAD_W], preferred_element_type=F32)

    aq[...] = (proj(0) * q_scale).astype(BF16)
    ak[...] = proj(1).astype(BF16)
    av[...] = proj(2).astype(BF16)
    ag[...] = proj(3)
    bq[...] = (proj(4) * q_scale).astype(BF16)
    bk[...] = proj(5).astype(BF16)
    bv[...] = proj(6).astype(BF16)
    bg[...] = proj(7)
    tail = proj(8)
    iq[...] = tail[:, 0:256].astype(BF16)
    ik[...] = tail[:, 256:384].astype(BF16)
    iw[...] = tail[:, 384:512] * idx_scale


def _inproj(x2, norm_w, w_pad):
    n, d = x2.shape
    grid = (n // PROJ_TM,)
    row = lambda w: pl.BlockSpec((PROJ_TM, w), lambda i: (i, 0))
    out_shape = ([jax.ShapeDtypeStruct((n, HEAD_W), BF16)] * 3 + [jax.ShapeDtypeStruct((n, HEAD_W), F32)]) * 2 + [
        jax.ShapeDtypeStruct((n, 256), BF16), jax.ShapeDtypeStruct((n, LANES), BF16),
        jax.ShapeDtypeStruct((n, LANES), F32)]
    out_specs = [row(HEAD_W)] * 8 + [row(256), row(LANES), row(LANES)]
    kern = functools.partial(_inproj_kernel, q_scale=A_QK_DIM ** -0.5,
                             idx_scale=(IDX_HEADS ** -0.5) * (IDX_DIM ** -0.5))
    return pl.pallas_call(
        kern, grid=grid,
        in_specs=[row(d), pl.BlockSpec((1, d), lambda i: (0, 0)),
                  pl.BlockSpec(w_pad.shape, lambda i: (0, 0))],
        out_specs=out_specs, out_shape=out_shape,
        compiler_params=pltpu.CompilerParams(dimension_semantics=("arbitrary",),
                                             vmem_limit_bytes=VMEM_LIMIT),
        name="inproj",
    )(x2, norm_w.reshape(1, d), w_pad)


def _bias_kernel(tab_ref, out_ref):
    h = pl.program_id(0)
    t = lax.broadcasted_iota(I32, (KV_TILE, KV_TILE), 0)
    s = lax.broadcasted_iota(I32, (KV_TILE, KV_TILE), 1)
    far = tab_ref[N_BUCKETS // 2 - 1, h]
    for kind in (0, 1):
        rel = s - t - KV_TILE * kind
        n = jnp.abs(rel)
        large = jnp.full_like(n, N_BUCKETS // 4)
        for thr in _T5_THRESHOLDS:
            large = large + (n >= thr).astype(I32)
        bucket = jnp.where(rel > 0, N_BUCKETS // 2, 0) + jnp.where(n < N_BUCKETS // 4, n, large)
        bias = jnp.zeros((KV_TILE, KV_TILE), F32)
        for b in range(N_BUCKETS):
            bias = jnp.where(bucket == b, tab_ref[b, h], bias)
        bias = bias - far
        if kind == 0:
            blocked = jnp.logical_and(h < A_HEADS, (s // CHUNK) > (t // CHUNK))
            bias = jnp.where(blocked, NEG_INF, bias)
        out_ref[kind] = bias


def _bias_tiles(rel_bias):
    nh = rel_bias.shape[1]
    return pl.pallas_call(
        _bias_kernel, grid=(nh,),
        in_specs=[pl.BlockSpec(memory_space=pltpu.SMEM)],
        out_specs=pl.BlockSpec((None, 2, KV_TILE, KV_TILE), lambda h: (h, 0, 0, 0)),
        out_shape=jax.ShapeDtypeStruct((nh, 2, KV_TILE, KV_TILE), F32),
        name="bias_tiles",
    )(rel_bias)


def _stack_masked_queries(q, qm_scr):
    low = lax.broadcasted_iota(I32, (1, LANES), 1) < LANES // 2
    for p in range(N_UNITS // 2):
        qp = q[:, p * LANES:(p + 1) * LANES]
        zero = jnp.zeros_like(qp)
        qm_scr[p, 0:ROWS, 0:LANES] = jnp.where(low, qp, zero)
        qm_scr[p, ROWS:2 * ROWS, 0:LANES] = jnp.where(low, zero, qp)
    if qm_scr.shape[-1] == 2 * LANES:
        hit = (lax.broadcasted_iota(I32, (ROWS, LANES), 0)
               == lax.broadcasted_iota(I32, (ROWS, LANES), 1))
        eye = jnp.where(hit, MASKED_LOGIT, 0.0).astype(BF16)
        for p in range(N_UNITS // 2):
            qm_scr[p, 0:ROWS, LANES:2 * LANES] = eye
            qm_scr[p, ROWS:2 * ROWS, LANES:2 * LANES] = eye


def _two_pass_attention(i, r0, qm_scr, k_ref, v_ref, bias_ref, bias_head, unsel_scr,
                        s_scr, mx_scr, l_scr, acc_scr):
    n_tiles = i + 1
    n_far = jnp.maximum(i - 1, 0)
    mx_scr[...] = jnp.full(mx_scr.shape, NEG_INF, F32)
    l_scr[...] = jnp.zeros(l_scr.shape, F32)
    acc_scr[...] = jnp.zeros(acc_scr.shape, F32)

    def lane_groups(x):
        return [x[:, c * LANES:(c + 1) * LANES] for c in range(x.shape[1] // LANES)]

    def run_tiles(count, fn):
        def body(t, carry):
            fn(4 * t, 4 * KV_TILE)
            return carry
        lax.fori_loop(0, count // 4, body, 0)

        @pl.when(count % 4 >= 2)
        def _():
            fn(4 * (count // 4), 2 * KV_TILE)

        @pl.when(count % 2 == 1)
        def _():
            fn(count - 1, KV_TILE)

    def logits_tile(j, width, bias_kinds=()):
        start = pl.multiple_of(j * KV_TILE, KV_TILE)
        for p in range(N_UNITS // 2):
            kc = k_ref[pl.ds(start, width), p * LANES:(p + 1) * LANES]
            if unsel_scr is not None:
                kc = jnp.concatenate([kc, unsel_scr[pl.ds(start, width), :]], axis=1)
            s2 = _dot_nt(qm_scr[p], kc)
            for half in range(2):
                u = 2 * p + half
                s = s2[half * ROWS:(half + 1) * ROWS]
                if bias_kinds:
                    s = s + jnp.concatenate(
                        [bias_ref[bias_head(u), kind, pl.ds(r0, ROWS), :] for kind in bias_kinds],
                        axis=1)
                s_scr[u, :, pl.ds(start, width)] = s
                mx_scr[u] = jnp.maximum(mx_scr[u], functools.reduce(jnp.maximum, lane_groups(s)))

    run_tiles(n_far, logits_tile)

    @pl.when(i >= 1)
    def _():
        logits_tile(i - 1, 2 * KV_TILE, (1, 0))

    @pl.when(i == 0)
    def _():
        logits_tile(0, KV_TILE, (0,))

    for u in range(N_UNITS):
        m = jnp.max(mx_scr[u], axis=1, keepdims=True)
        mx_scr[u] = jnp.broadcast_to(m, (ROWS, LANES))

    def exp_tile(j, width):
        start = pl.multiple_of(j * KV_TILE, KV_TILE)
        for u in range(N_UNITS):
            m = mx_scr[u]
            ps = [jnp.exp(sg - m) for sg in lane_groups(s_scr[u, :, pl.ds(start, width)])]
            l_scr[u] = l_scr[u] + functools.reduce(lambda a, b: a + b, ps)
            pb = jnp.concatenate(ps, axis=1).astype(BF16)
            g = u // 2
            vc = v_ref[pl.ds(start, width), g * LANES:(g + 1) * LANES]
            acc_scr[u] = acc_scr[u] + jnp.dot(pb, vc, preferred_element_type=F32)

    run_tiles(n_tiles, exp_tile)


def _normalised(u, l_scr, acc_scr):
    return acc_scr[u] / jnp.sum(l_scr[u], axis=1, keepdims=True)


def _tree_sum(x):
    parts = [x[t] for t in range(x.shape[0])]
    while len(parts) > 1:
        parts = [parts[t] + parts[t + 1] for t in range(0, len(parts) - 1, 2)] + (
            [parts[-1]] if len(parts) % 2 else [])
    return parts[0]


def _silu(g):
    return g * (1.0 / (1.0 + jnp.exp(-g)))


def _attention_scratch(s, q_width=LANES):
    return [pltpu.VMEM((N_UNITS // 2, 2 * ROWS, q_width), BF16),
            pltpu.VMEM((N_UNITS, ROWS, s), F32),
            pltpu.VMEM((N_UNITS, ROWS, LANES), F32),
            pltpu.VMEM((N_UNITS, ROWS, LANES), F32),
            pltpu.VMEM((N_UNITS, ROWS, LANES), F32)]


def _diff_kernel(lq1, lk1, lq2, lk2, subw_ref, q_ref, k_ref, v_ref, g_ref, bias_ref, o_ref,
                 qm_scr, s_scr, mx_scr, l_scr, acc_scr, *, lambda_init):
    tb = pl.program_id(1)
    i = tb // 2
    r0 = pl.multiple_of((tb % 2) * ROWS, ROWS)
    lam = (jnp.exp(jnp.sum(lq1[...] * lk1[...], axis=-1, keepdims=True))
           - jnp.exp(jnp.sum(lq2[...] * lk2[...], axis=-1, keepdims=True)) + lambda_init)
    _stack_masked_queries(q_ref[...], qm_scr)
    _two_pass_attention(i, r0, qm_scr, k_ref, v_ref, bias_ref, lambda u: u // 2, None,
                        s_scr, mx_scr, l_scr, acc_scr)
    for h in range(A_HEADS):
        o = _normalised(2 * h, l_scr, acc_scr) - lam * _normalised(2 * h + 1, l_scr, acc_scr)
        y = (o * lax.rsqrt(jnp.mean(o * o, axis=-1, keepdims=True) + EPS)) * subw_ref[...]
        y = y * (1.0 - lambda_init)
        cols = slice(h * LANES, (h + 1) * LANES)
        o_ref[:, cols] = (y * _silu(g_ref[:, cols])).astype(BF16)


def _diff_attention(aq, ak, av, ag, bias, lq1, lk1, lq2, lk2, subln_w, lambda_init):
    b, s, _ = aq.shape
    grid = (b, s // ROWS)
    vec = lambda a: pl.BlockSpec((1, a.shape[-1]), lambda bi, t: (0, 0))
    rows = pl.BlockSpec((None, ROWS, HEAD_W), lambda bi, t: (bi, t, 0))
    full = _resident((None, s, HEAD_W), lambda bi, t: (bi, 0, 0))
    kern = functools.partial(_diff_kernel, lambda_init=lambda_init)
    return pl.pallas_call(
        kern, grid=grid,
        in_specs=[vec(lq1), vec(lk1), vec(lq2), vec(lk2), vec(subln_w), rows, full, full, rows,
                  _resident(bias.shape, lambda bi, t: (0, 0, 0, 0))],
        out_specs=rows,
        out_shape=jax.ShapeDtypeStruct((b, s, HEAD_W), BF16),
        scratch_shapes=_attention_scratch(s),
        compiler_params=pltpu.CompilerParams(dimension_semantics=("arbitrary",) * 2,
                                             vmem_limit_bytes=VMEM_LIMIT),
        name="diff_attention",
    )(lq1, lk1, lq2, lk2, subln_w, aq, ak, av, ag, bias)


def _dsa_kernel(iq_ref, ik_ref, iw_ref, q_ref, k_ref, v_ref, g_ref, bias_ref, o_ref,
                sc_scr, iqm_scr, unsel_scr, qm_scr, s_scr, mx_scr, l_scr, acc_scr, *, k_sel):
    tb = pl.program_id(1)
    i = tb // 2
    n_tiles = i + 1
    n_scan = (n_tiles + 1) // 2
    r0 = pl.multiple_of((tb % 2) * ROWS, ROWS)
    lane = lax.broadcasted_iota(I32, (1, LANES), 1)
    low = lane < B_HEAD_DIM
    q_chunk = (tb * ROWS + lane) // CHUNK
    key_in_tile = lax.broadcasted_iota(I32, (KV_TILE, LANES), 0)

    iq = iq_ref[...]
    for hh in range(IDX_HEADS):
        pair = iq[:, (hh // 2) * LANES:(hh // 2 + 1) * LANES]
        iqm_scr[hh * ROWS:(hh + 1) * ROWS, :] = jnp.where(low == (hh % 2 == 0), pair,
                                                          jnp.zeros_like(pair))
    iw_t = iw_ref[...].T

    def score_tile(j, diagonal):
        start = pl.multiple_of(j * KV_TILE, KV_TILE)
        logits = _dot_nt(ik_ref[pl.ds(start, KV_TILE), :], iqm_scr[...])
        sc = jnp.zeros((KV_TILE, LANES), F32)
        for hh in range(IDX_HEADS):
            sc = sc + iw_t[hh:hh + 1, :] * jnp.maximum(logits[:, hh * ROWS:(hh + 1) * ROWS], 0.0)
        if diagonal:
            sc = jnp.where(((start + key_in_tile) // CHUNK) <= q_chunk, sc, NEG_INF)
        sc_scr[pl.ds(start, KV_TILE), :] = sc

    def far_scores(t, carry):
        score_tile(2 * t, False)
        score_tile(2 * t + 1, False)
        return carry

    lax.fori_loop(0, i // 2, far_scores, 0)

    @pl.when(i % 2 == 1)
    def _():
        score_tile(i - 1, False)

    score_tile(i, True)

    @pl.when(n_tiles % 2 == 1)
    def _():
        pad = pl.ds(pl.multiple_of(n_tiles * KV_TILE, KV_TILE), KV_TILE)
        sc_scr[pad, :] = jnp.full((KV_TILE, LANES), NEG_INF, F32)

    def key_to_float(key):
        bits = key ^ (lax.shift_right_arithmetic(key, 31) & 0x7FFFFFFF)
        return lax.bitcast_convert_type(bits, F32)

    def scan_count(pred):
        def body(t, accs):
            blk = sc_scr[pl.ds(pl.multiple_of(t * SCAN_TILE, SCAN_TILE), SCAN_TILE), :]
            hit = pred(blk.reshape(SCAN_TILE // 8, 8, LANES))
            accs = list(accs)
            for v in range(SCAN_TILE // 8):
                c = v % len(accs)
                accs[c] = jnp.where(hit[v], accs[c] + 1.0, accs[c])
            return tuple(accs)
        zero = jnp.zeros((8, LANES), F32)
        accs = lax.fori_loop(0, n_scan, body, (zero,) * SCAN_CHAINS)
        return jnp.sum(functools.reduce(lambda a, b: a + b, accs), axis=0, keepdims=True)

    kf = float(k_sel)

    def bit_step(b, carry):
        key, c_key = carry
        cand = key + lax.shift_left(jnp.int32(1), 31 - b)
        cand_f = key_to_float(cand)
        c = scan_count(lambda blk: blk >= cand_f)
        ok = jnp.logical_or(c >= kf, cand <= KEY_NEG_INF)
        return jnp.where(ok, cand, key), jnp.where(ok, c, c_key)

    n_keys = jnp.full((1, LANES), 1.0, F32) * (n_tiles * KV_TILE).astype(F32)
    thr_key, c_thr = lax.fori_loop(0, 32, bit_step,
                                   (jnp.full((1, LANES), INT_MIN, I32), n_keys))
    thr = key_to_float(thr_key)

    def tile_count(pred):
        def body(j, acc):
            start = pl.multiple_of(j * KV_TILE, KV_TILE)
            hit = jnp.where(pred(sc_scr[pl.ds(start, KV_TILE), :], start + key_in_tile), 1.0, 0.0)
            return acc + _tree_sum(hit.reshape(KV_TILE // 8, 8, LANES))
        acc = lax.fori_loop(0, n_tiles, body, jnp.zeros((8, LANES), F32))
        return jnp.sum(acc, axis=0, keepdims=True)

    def tie_limit():
        need = kf - tile_count(lambda kc, idx: kc > thr)

        def idx_step(b, lim):
            cand = lim + lax.shift_left(jnp.int32(1), 11 - b)
            f = tile_count(lambda kc, idx: jnp.logical_and(kc == thr, idx < cand))
            return jnp.where(f < need, cand, lim)

        return lax.fori_loop(0, 12, idx_step, jnp.zeros((1, LANES), I32))

    lim = lax.cond(jnp.max(c_thr) > kf, tie_limit,
                   lambda: jnp.full((1, LANES), 2 ** 30, I32))

    def mask_tile(j, diagonal):
        start = pl.multiple_of(j * KV_TILE, KV_TILE)
        kc = sc_scr[pl.ds(start, KV_TILE), :]
        idx = start + key_in_tile
        sel = jnp.logical_or(kc > thr, jnp.logical_and(kc == thr, idx <= lim))
        if diagonal:
            sel = jnp.logical_and(sel, (idx // CHUNK) <= q_chunk)
        unsel_scr[pl.ds(start, KV_TILE), :] = jnp.where(sel, 0.0, 1.0).astype(BF16)

    def far_mask(j, carry):
        mask_tile(j, False)
        return carry

    lax.fori_loop(0, i, far_mask, 0)
    mask_tile(i, True)

    _stack_masked_queries(q_ref[...], qm_scr)
    _two_pass_attention(i, r0, qm_scr, k_ref, v_ref, bias_ref, lambda u: u, unsel_scr,
                        s_scr, mx_scr, l_scr, acc_scr)
    for p in range(B_HEADS // 2):
        o = jnp.where(low, _normalised(2 * p, l_scr, acc_scr), _normalised(2 * p + 1, l_scr, acc_scr))
        cols = slice(p * LANES, (p + 1) * LANES)
        o_ref[:, cols] = (o * _silu(g_ref[:, cols])).astype(BF16)


def _dsa_attention(iq, ik, iw, bq, bk, bv, bg, bias, k_sel):
    b, s, _ = bq.shape
    grid = (b, s // ROWS)
    rows = lambda w: pl.BlockSpec((None, ROWS, w), lambda bi, t: (bi, t, 0))
    full = lambda w: _resident((None, s, w), lambda bi, t: (bi, 0, 0))
    kern = functools.partial(_dsa_kernel, k_sel=k_sel)
    return pl.pallas_call(
        kern, grid=grid,
        in_specs=[rows(256), full(LANES), rows(LANES), rows(HEAD_W), full(HEAD_W), full(HEAD_W),
                  rows(HEAD_W), _resident(bias.shape, lambda bi, t: (0, 0, 0, 0))],
        out_specs=rows(HEAD_W),
        out_shape=jax.ShapeDtypeStruct((b, s, HEAD_W), BF16),
        scratch_shapes=[pltpu.VMEM((s, LANES), F32),
                        pltpu.VMEM((IDX_HEADS * ROWS, LANES), BF16),
                        pltpu.VMEM((s, LANES), BF16)]
        + _attention_scratch(s, 2 * LANES),
        compiler_params=pltpu.CompilerParams(dimension_semantics=("arbitrary",) * 2,
                                             vmem_limit_bytes=VMEM_LIMIT),
        name="dsa_attention",
    )(iq, ik, iw, bq, bk, bv, bg, bias)


def _outproj_kernel(ao_ref, bo_ref, x_ref, w_ref, fw_ref, o_ref):
    y = jnp.dot(ao_ref[...], w_ref[0:HEAD_W, :], preferred_element_type=F32)
    y = y + jnp.dot(bo_ref[...], w_ref[HEAD_W:2 * HEAD_W, :], preferred_element_type=F32)
    z = x_ref[...] + y
    o_ref[...] = (z * lax.rsqrt(jnp.mean(z * z, axis=-1, keepdims=True) + EPS)) * fw_ref[...]


def _outproj(ao, bo, x2, w_out, final_w):
    n, d = x2.shape
    row = lambda w: pl.BlockSpec((PROJ_TM, w), lambda i: (i, 0))
    return pl.pallas_call(
        _outproj_kernel, grid=(n // PROJ_TM,),
        in_specs=[row(HEAD_W), row(HEAD_W), row(d), pl.BlockSpec(w_out.shape, lambda i: (0, 0)),
                  pl.BlockSpec((1, d), lambda i: (0, 0))],
        out_specs=row(d), out_shape=jax.ShapeDtypeStruct((n, d), F32),
        compiler_params=pltpu.CompilerParams(dimension_semantics=("arbitrary",),
                                             vmem_limit_bytes=VMEM_LIMIT),
        name="outproj",
    )(ao, bo, x2, w_out, final_w.reshape(1, d))


def kernel(x, norm_w, w_in, w_out, lambda_q1, lambda_k1, lambda_q2, lambda_k2, subln_w, rel_bias,
           final_norm_w):
    b, s, d = x.shape
    depth = norm_w.shape[0]
    assert depth == 1, "single-layer trunk"
    assert s % KV_TILE == 0 and (b * s) % PROJ_TM == 0
    layer = 0
    lambda_init = 0.8 - 0.6 * math.exp(-0.3 * layer)
    k_sel = min(TOPK_MAX, s // 4)

    w = w_in[layer]
    main = 8 * HEAD_W
    iq_w = w[:, main:main + IDX_HEADS * IDX_DIM]
    ik_w = w[:, main + IDX_HEADS * IDX_DIM:main + IDX_HEADS * IDX_DIM + IDX_DIM]
    iw_w = w[:, main + IDX_HEADS * IDX_DIM + IDX_DIM:]
    pad = jnp.zeros((d, LANES - IDX_HEADS), w.dtype)
    w_pad = jnp.concatenate([w[:, :main], iq_w, ik_w, ik_w, iw_w, pad], axis=1).astype(BF16)

    x2 = x.reshape(b * s, d)
    aq, ak, av, ag, bq, bk, bv, bg, iq, ik, iw = _inproj(x2, norm_w[layer], w_pad)
    r3 = lambda a: a.reshape(b, s, a.shape[-1])
    bias = _bias_tiles(rel_bias)

    vec = lambda a: a.reshape(1, -1)
    ao = _diff_attention(r3(aq), r3(ak), r3(av), r3(ag), bias[:A_HEADS],
                         vec(lambda_q1[layer]), vec(lambda_k1[layer]),
                         vec(lambda_q2[layer]), vec(lambda_k2[layer]), vec(subln_w[layer]),
                         lambda_init)
    bo = _dsa_attention(r3(iq), r3(ik), r3(iw), r3(bq), r3(bk), r3(bv), r3(bg),
                        bias[A_HEADS:], k_sel)

    out = _outproj(ao.reshape(b * s, -1), bo.reshape(b * s, -1), x2,
                   w_out[layer].astype(BF16), final_norm_w)
    return out.reshape(b, s, d)
```

```python
import functools
import math

import jax
import jax.numpy as jnp
from jax import lax
from jax.experimental import pallas as pl
from jax.experimental.pallas import tpu as pltpu

F32 = jnp.float32
BF16 = jnp.bfloat16
I32 = jnp.int32

EPS = 1e-6
CHUNK = 64
A_HEADS = 4
A_QK_DIM = 64
A_V_DIM = 128
B_HEADS = 8
B_HEAD_DIM = 64
IDX_HEADS = 4
IDX_DIM = 64
TOPK_MAX = 256
N_BUCKETS = 32
HEAD_W = 512
LANES = 128
ROWS = 128
KV_TILE = 256
N_UNITS = 8
PROJ_TM = 512
SCAN_TILE = 512
SCAN_CHAINS = 8
INT_MIN = -2 ** 31
KEY_NEG_INF = INT_MIN + 0x7FFFFF
NEG_INF = float("-inf")
MASKED_LOGIT = -(2.0 ** 120)
VMEM_LIMIT = 56 * 1024 * 1024

_T5_THRESHOLDS = (12, 16, 23, 32, 46, 64, 91)


def _dot_nt(a, b):
    return lax.dot_general(a, b, (((1,), (1,)), ((), ())), preferred_element_type=F32)


def _resident(shape, index_map):
    return pl.BlockSpec(shape, index_map, pipeline_mode=pl.Buffered(1))


def _inproj_kernel(x_ref, nw_ref, w_ref, aq, ak, av, ag, bq, bk, bv, bg, iq, ik, iw,
                   *, q_scale, idx_scale):
    x = x_ref[...]
    ms = jnp.mean(x * x, axis=-1, keepdims=True)
    h = ((x * lax.rsqrt(ms + EPS)) * nw_ref[...]).astype(BF16)

    def proj(c):
        return jnp.dot(h, w_ref[:, c * HEAD_W:(c + 1) * HEAD_W], preferred_element_type=F32)

    aq[...] = (proj(0) * q_scale).astype(BF16)
    ak[...] = proj(1).astype(BF16)
    av[...] = proj(2).astype(BF16)
    ag[...] = proj(3)
    bq[...] = (proj(4) * q_scale).astype(BF16)
    bk[...] = proj(5).astype(BF16)
    bv[...] = proj(6).astype(BF16)
    bg[...] = proj(7)
    tail = proj(8)
    iq[...] = tail[:, 0:256].astype(BF16)
    ik[...] = tail[:, 256:384].astype(BF16)
    iw[...] = tail[:, 384:512] * idx_scale


def _inproj(x2, norm_w, w_pad):
    n, d = x2.shape
    grid = (n // PROJ_TM,)
    row = lambda w: pl.BlockSpec((PROJ_TM, w), lambda i: (i, 0))
    out_shape = ([jax.ShapeDtypeStruct((n, HEAD_W), BF16)] * 3 + [jax.ShapeDtypeStruct((n, HEAD_W), F32)]) * 2 + [
        jax.ShapeDtypeStruct((n, 256), BF16), jax.ShapeDtypeStruct((n, LANES), BF16),
        jax.ShapeDtypeStruct((n, LANES), F32)]
    out_specs = [row(HEAD_W)] * 8 + [row(256), row(LANES), row(LANES)]
    kern = functools.partial(_inproj_kernel, q_scale=A_QK_DIM ** -0.5,
                             idx_scale=(IDX_HEADS ** -0.5) * (IDX_DIM ** -0.5))
    return pl.pallas_call(
        kern, grid=grid,
        in_specs=[row(d), pl.BlockSpec((1, d), lambda i: (0, 0)),
                  pl.BlockSpec(w_pad.shape, lambda i: (0, 0))],
        out_specs=out_specs, out_shape=out_shape,
        compiler_params=pltpu.CompilerParams(dimension_semantics=("arbitrary",),
                                             vmem_limit_bytes=VMEM_LIMIT),
        name="inproj",
    )(x2, norm_w.reshape(1, d), w_pad)


def _bias_kernel(tab_ref, out_ref):
    h = pl.program_id(0)
    t = lax.broadcasted_iota(I32, (KV_TILE, KV_TILE), 0)
    s = lax.broadcasted_iota(I32, (KV_TILE, KV_TILE), 1)
    far = tab_ref[N_BUCKETS // 2 - 1, h]
    for kind in (0, 1):
        rel = s - t - KV_TILE * kind
        n = jnp.abs(rel)
        large = jnp.full_like(n, N_BUCKETS // 4)
        for thr in _T5_THRESHOLDS:
            large = large + (n >= thr).astype(I32)
        bucket = jnp.where(rel > 0, N_BUCKETS // 2, 0) + jnp.where(n < N_BUCKETS // 4, n, large)
        bias = jnp.zeros((KV_TILE, KV_TILE), F32)
        for b in range(N_BUCKETS):
            bias = jnp.where(bucket == b, tab_ref[b, h], bias)
        bias = bias - far
        if kind == 0:
            blocked = jnp.logical_and(h < A_HEADS, (s // CHUNK) > (t // CHUNK))
            bias = jnp.where(blocked, NEG_INF, bias)
        out_ref[kind] = bias


def _bias_tiles(rel_bias):
    nh = rel_bias.shape[1]
    return pl.pallas_call(
        _bias_kernel, grid=(nh,),
        in_specs=[pl.BlockSpec(memory_space=pltpu.SMEM)],
        out_specs=pl.BlockSpec((None, 2, KV_TILE, KV_TILE), lambda h: (h, 0, 0, 0)),
        out_shape=jax.ShapeDtypeStruct((nh, 2, KV_TILE, KV_TILE), F32),
        name="bias_tiles",
    )(rel_bias)


def _stack_masked_queries(q, qm_scr):
    low = lax.broadcasted_iota(I32, (1, LANES), 1) < LANES // 2
    for p in range(N_UNITS // 2):
        qp = q[:, p * LANES:(p + 1) * LANES]
        zero = jnp.zeros_like(qp)
        qm_scr[p, 0:ROWS, 0:LANES] = jnp.where(low, qp, zero)
        qm_scr[p, ROWS:2 * ROWS, 0:LANES] = jnp.where(low, zero, qp)
    if qm_scr.shape[-1] == 2 * LANES:
        hit = (lax.broadcasted_iota(I32, (ROWS, LANES), 0)
               == lax.broadcasted_iota(I32, (ROWS, LANES), 1))
        eye = jnp.where(hit, MASKED_LOGIT, 0.0).astype(BF16)
        for p in range(N_UNITS // 2):
            qm_scr[p, 0:ROWS, LANES:2 * LANES] = eye
            qm_scr[p, ROWS:2 * ROWS, LANES:2 * LANES] = eye


def _two_pass_attention(i, r0, qm_scr, k_ref, v_ref, bias_ref, bias_head, unsel_scr,
                        s_scr, mx_scr, l_scr, acc_scr):
    n_tiles = i + 1
    n_far = jnp.maximum(i - 1, 0)
    mx_scr[...] = jnp.full(mx_scr.shape, NEG_INF, F32)
    l_scr[...] = jnp.zeros(l_scr.shape, F32)
    acc_scr[...] = jnp.zeros(acc_scr.shape, F32)

    def lane_groups(x):
        return [x[:, c * LANES:(c + 1) * LANES] for c in range(x.shape[1] // LANES)]

    def run_tiles(count, fn):
        def body(t, carry):
            fn(4 * t, 4 * KV_TILE)
            return carry
        lax.fori_loop(0, count // 4, body, 0)

        @pl.when(count % 4 >= 2)
        def _():
            fn(4 * (count // 4), 2 * KV_TILE)

        @pl.when(count % 2 == 1)
        def _():
            fn(count - 1, KV_TILE)

    def logits_tile(j, width, bias_kinds=()):
        start = pl.multiple_of(j * KV_TILE, KV_TILE)
        for p in range(N_UNITS // 2):
            kc = k_ref[pl.ds(start, width), p * LANES:(p + 1) * LANES]
            if unsel_scr is not None:
                kc = jnp.concatenate([kc, unsel_scr[pl.ds(start, width), :]], axis=1)
            s2 = _dot_nt(qm_scr[p], kc)
            for half in range(2):
                u = 2 * p + half
                s = s2[half * ROWS:(half + 1) * ROWS]
                if bias_kinds:
                    s = s + jnp.concatenate(
                        [bias_ref[bias_head(u), kind, pl.ds(r0, ROWS), :] for kind in bias_kinds],
                        axis=1)
                s_scr[u, :, pl.ds(start, width)] = s
                mx_scr[u] = jnp.maximum(mx_scr[u], functools.reduce(jnp.maximum, lane_groups(s)))

    run_tiles(n_far, logits_tile)

    @pl.when(i >= 1)
    def _():
        logits_tile(i - 1, 2 * KV_TILE, (1, 0))

    @pl.when(i == 0)
    def _():
        logits_tile(0, KV_TILE, (0,))

    for u in range(N_UNITS):
        m = jnp.max(mx_scr[u], axis=1, keepdims=True)
        mx_scr[u] = jnp.broadcast_to(m, (ROWS, LANES))

    def exp_tile(j, width):
        start = pl.multiple_of(j * KV_TILE, KV_TILE)
        for u in range(N_UNITS):
            m = mx_scr[u]
            ps = [jnp.exp(sg - m) for sg in lane_groups(s_scr[u, :, pl.ds(start, width)])]
            l_scr[u] = l_scr[u] + functools.reduce(lambda a, b: a + b, ps)
            pb = jnp.concatenate(ps, axis=1).astype(BF16)
            g = u // 2
            vc = v_ref[pl.ds(start, width), g * LANES:(g + 1) * LANES]
            acc_scr[u] = acc_scr[u] + jnp.dot(pb, vc, preferred_element_type=F32)

    run_tiles(n_tiles, exp_tile)


def _normalised(u, l_scr, acc_scr):
    return acc_scr[u] / jnp.sum(l_scr[u], axis=1, keepdims=True)


def _tree_sum(x):
    parts = [x[t] for t in range(x.shape[0])]
    while len(parts) > 1:
        parts = [parts[t] + parts[t + 1] for t in range(0, len(parts) - 1, 2)] + (
            [parts[-1]] if len(parts) % 2 else [])
    return parts[0]


def _silu(g):
    return g * (1.0 / (1.0 + jnp.exp(-g)))


def _attention_scratch(s, q_width=LANES):
    return [pltpu.VMEM((N_UNITS // 2, 2 * ROWS, q_width), BF16),
            pltpu.VMEM((N_UNITS, ROWS, s), F32),
            pltpu.VMEM((N_UNITS, ROWS, LANES), F32),
            pltpu.VMEM((N_UNITS, ROWS, LANES), F32),
            pltpu.VMEM((N_UNITS, ROWS, LANES), F32)]


def _diff_kernel(lq1, lk1, lq2, lk2, subw_ref, q_ref, k_ref, v_ref, g_ref, bias_ref, o_ref,
                 qm_scr, s_scr, mx_scr, l_scr, acc_scr, *, lambda_init):
    tb = pl.program_id(1)
    i = tb // 2
    r0 = pl.multiple_of((tb % 2) * ROWS, ROWS)
    lam = (jnp.exp(jnp.sum(lq1[...] * lk1[...], axis=-1, keepdims=True))
           - jnp.exp(jnp.sum(lq2[...] * lk2[...], axis=-1, keepdims=True)) + lambda_init)
    _stack_masked_queries(q_ref[...], qm_scr)
    _two_pass_attention(i, r0, qm_scr, k_ref, v_ref, bias_ref, lambda u: u // 2, None,
                        s_scr, mx_scr, l_scr, acc_scr)
    for h in range(A_HEADS):
        o = _normalised(2 * h, l_scr, acc_scr) - lam * _normalised(2 * h + 1, l_scr, acc_scr)
        y = (o * lax.rsqrt(jnp.mean(o * o, axis=-1, keepdims=True) + EPS)) * subw_ref[...]
        y = y * (1.0 - lambda_init)
        cols = slice(h * LANES, (h + 1) * LANES)
        o_ref[:, cols] = (y * _silu(g_ref[:, cols])).astype(BF16)


def _diff_attention(aq, ak, av, ag, bias, lq1, lk1, lq2, lk2, subln_w, lambda_init):
    b, s, _ = aq.shape
    grid = (b, s // ROWS)
    vec = lambda a: pl.BlockSpec((1, a.shape[-1]), lambda bi, t: (0, 0))
    rows = pl.BlockSpec((None, ROWS, HEAD_W), lambda bi, t: (bi, t, 0))
    full = _resident((None, s, HEAD_W), lambda bi, t: (bi, 0, 0))
    kern = functools.partial(_diff_kernel, lambda_init=lambda_init)
    return pl.pallas_call(
        kern, grid=grid,
        in_specs=[vec(lq1), vec(lk1), vec(lq2), vec(lk2), vec(subln_w), rows, full, full, rows,
                  _resident(bias.shape, lambda bi, t: (0, 0, 0, 0))],
        out_specs=rows,
        out_shape=jax.ShapeDtypeStruct((b, s, HEAD_W), BF16),
        scratch_shapes=_attention_scratch(s),
        compiler_params=pltpu.CompilerParams(dimension_semantics=("arbitrary",) * 2,
                                             vmem_limit_bytes=VMEM_LIMIT),
        name="diff_attention",
    )(lq1, lk1, lq2, lk2, subln_w, aq, ak, av, ag, bias)


def _dsa_kernel(iq_ref, ik_ref, iw_ref, q_ref, k_ref, v_ref, g_ref, bias_ref, o_ref,
                sc_scr, iqm_scr, unsel_scr, qm_scr, s_scr, mx_scr, l_scr, acc_scr, *, k_sel):
    tb = pl.program_id(1)
    i = tb // 2
    n_tiles = i + 1
    n_scan = (n_tiles + 1) // 2
    r0 = pl.multiple_of((tb % 2) * ROWS, ROWS)
    lane = lax.broadcasted_iota(I32, (1, LANES), 1)
    low = lane < B_HEAD_DIM
    q_chunk = (tb * ROWS + lane) // CHUNK
    key_in_tile = lax.broadcasted_iota(I32, (KV_TILE, LANES), 0)

    iq = iq_ref[...]
    for hh in range(IDX_HEADS):
        pair = iq[:, (hh // 2) * LANES:(hh // 2 + 1) * LANES]
        iqm_scr[hh * ROWS:(hh + 1) * ROWS, :] = jnp.where(low == (hh % 2 == 0), pair,
                                                          jnp.zeros_like(pair))
    iw_t = iw_ref[...].T

    def score_tile(j, diagonal):
        start = pl.multiple_of(j * KV_TILE, KV_TILE)
        logits = _dot_nt(ik_ref[pl.ds(start, KV_TILE), :], iqm_scr[...])
        sc = jnp.zeros((KV_TILE, LANES), F32)
        for hh in range(IDX_HEADS):
            sc = sc + iw_t[hh:hh + 1, :] * jnp.maximum(logits[:, hh * ROWS:(hh + 1) * ROWS], 0.0)
        if diagonal:
            sc = jnp.where(((start + key_in_tile) // CHUNK) <= q_chunk, sc, NEG_INF)
        sc_scr[pl.ds(start, KV_TILE), :] = sc

    def far_scores(t, carry):
        for d in range(4):
            score_tile(4 * t + d, False)
        return carry

    lax.fori_loop(0, i // 4, far_scores, 0)

    @pl.when((i // 2) % 2 == 1)
    def _():
        score_tile(4 * (i // 4), False)
        score_tile(4 * (i // 4) + 1, False)

    @pl.when(i % 2 == 1)
    def _():
        score_tile(i - 1, False)
        score_tile(i, True)

    @pl.when(i % 2 == 0)
    def _():
        score_tile(i, True)

    @pl.when(n_tiles % 2 == 1)
    def _():
        pad = pl.ds(pl.multiple_of(n_tiles * KV_TILE, KV_TILE), KV_TILE)
        sc_scr[pad, :] = jnp.full((KV_TILE, LANES), NEG_INF, F32)

    def key_to_float(key):
        bits = key ^ (lax.shift_right_arithmetic(key, 31) & 0x7FFFFFFF)
        return lax.bitcast_convert_type(bits, F32)

    def scan_count(pred):
        def body(t, accs):
            blk = sc_scr[pl.ds(pl.multiple_of(t * SCAN_TILE, SCAN_TILE), SCAN_TILE), :]
            hit = pred(blk.reshape(SCAN_TILE // 8, 8, LANES))
            accs = list(accs)
            for v in range(SCAN_TILE // 8):
                c = v % len(accs)
                accs[c] = jnp.where(hit[v], accs[c] + 1.0, accs[c])
            return tuple(accs)
        zero = jnp.zeros((8, LANES), F32)
        accs = lax.fori_loop(0, n_scan, body, (zero,) * SCAN_CHAINS)
        return jnp.sum(functools.reduce(lambda a, b: a + b, accs), axis=0, keepdims=True)

    kf = float(k_sel)

    def bit_step(b, carry):
        key, c_key = carry
        cand = key + lax.shift_left(jnp.int32(1), 31 - b)
        cand_f = key_to_float(cand)
        c = scan_count(lambda blk: blk >= cand_f)
        ok = jnp.logical_or(c >= kf, cand <= KEY_NEG_INF)
        return jnp.where(ok, cand, key), jnp.where(ok, c, c_key)

    n_keys = jnp.full((1, LANES), 1.0, F32) * (n_tiles * KV_TILE).astype(F32)
    thr_key, c_thr = lax.fori_loop(0, 32, bit_step,
                                   (jnp.full((1, LANES), INT_MIN, I32), n_keys))
    thr = key_to_float(thr_key)

    def tile_count(pred):
        def body(j, acc):
            start = pl.multiple_of(j * KV_TILE, KV_TILE)
            hit = jnp.where(pred(sc_scr[pl.ds(start, KV_TILE), :], start + key_in_tile), 1.0, 0.0)
            return acc + _tree_sum(hit.reshape(KV_TILE // 8, 8, LANES))
        acc = lax.fori_loop(0, n_tiles, body, jnp.zeros((8, LANES), F32))
        return jnp.sum(acc, axis=0, keepdims=True)

    def tie_limit():
        need = kf - tile_count(lambda kc, idx: kc > thr)

        def idx_step(b, lim):
            cand = lim + lax.shift_left(jnp.int32(1), 11 - b)
            f = tile_count(lambda kc, idx: jnp.logical_and(kc == thr, idx < cand))
            return jnp.where(f < need, cand, lim)

        return lax.fori_loop(0, 12, idx_step, jnp.zeros((1, LANES), I32))

    lim = lax.cond(jnp.max(c_thr) > kf, tie_limit,
                   lambda: jnp.full((1, LANES), 2 ** 30, I32))

    def mask_tile(j, diagonal):
        start = pl.multiple_of(j * KV_TILE, KV_TILE)
        kc = sc_scr[pl.ds(start, KV_TILE), :]
        idx = start + key_in_tile
        sel = jnp.logical_or(kc > thr, jnp.logical_and(kc == thr, idx <= lim))
        if diagonal:
            sel = jnp.logical_and(sel, (idx // CHUNK) <= q_chunk)
        unsel_scr[pl.ds(start, KV_TILE), :] = jnp.where(sel, 0.0, 1.0).astype(BF16)

    def far_mask(j, carry):
        mask_tile(j, False)
        return carry

    lax.fori_loop(0, i, far_mask, 0)
    mask_tile(i, True)

    _stack_masked_queries(q_ref[...], qm_scr)
    _two_pass_attention(i, r0, qm_scr, k_ref, v_ref, bias_ref, lambda u: u, unsel_scr,
                        s_scr, mx_scr, l_scr, acc_scr)
    for p in range(B_HEADS // 2):
        o = jnp.where(low, _normalised(2 * p, l_scr, acc_scr), _normalised(2 * p + 1, l_scr, acc_scr))
        cols = slice(p * LANES, (p + 1) * LANES)
        o_ref[:, cols] = (o * _silu(g_ref[:, cols])).astype(BF16)


def _dsa_attention(iq, ik, iw, bq, bk, bv, bg, bias, k_sel):
    b, s, _ = bq.shape
    grid = (b, s // ROWS)
    rows = lambda w: pl.BlockSpec((None, ROWS, w), lambda bi, t: (bi, t, 0))
    full = lambda w: _resident((None, s, w), lambda bi, t: (bi, 0, 0))
    kern = functools.partial(_dsa_kernel, k_sel=k_sel)
    return pl.pallas_call(
        kern, grid=grid,
        in_specs=[rows(256), full(LANES), rows(LANES), rows(HEAD_W), full(HEAD_W), full(HEAD_W),
                  rows(HEAD_W), _resident(bias.shape, lambda bi, t: (0, 0, 0, 0))],
        out_specs=rows(HEAD_W),
        out_shape=jax.ShapeDtypeStruct((b, s, HEAD_W), BF16),
        scratch_shapes=[pltpu.VMEM((s, LANES), F32),
                        pltpu.VMEM((IDX_HEADS * ROWS, LANES), BF16),
                        pltpu.VMEM((s, LANES), BF16)]
        + _attention_scratch(s, 2 * LANES),
        compiler_params=pltpu.CompilerParams(dimension_semantics=("arbitrary",) * 2,
                                             vmem_limit_bytes=VMEM_LIMIT),
        name="dsa_attention",
    )(iq, ik, iw, bq, bk, bv, bg, bias)


def _outproj_kernel(ao_ref, bo_ref, x_ref, w_ref, fw_ref, o_ref):
    y = jnp.dot(ao_ref[...], w_ref[0:HEAD_W, :], preferred_element_type=F32)
    y = y + jnp.dot(bo_ref[...], w_ref[HEAD_W:2 * HEAD_W, :], preferred_element_type=F32)
    z = x_ref[...] + y
    o_ref[...] = (z * lax.rsqrt(jnp.mean(z * z, axis=-1, keepdims=True) + EPS)) * fw_ref[...]


def _outproj(ao, bo, x2, w_out, final_w):
    n, d = x2.shape
    row = lambda w: pl.BlockSpec((PROJ_TM, w), lambda i: (i, 0))
    return pl.pallas_call(
        _outproj_kernel, grid=(n // PROJ_TM,),
        in_specs=[row(HEAD_W), row(HEAD_W), row(d), pl.BlockSpec(w_out.shape, lambda i: (0, 0)),
                  pl.BlockSpec((1, d), lambda i: (0, 0))],
        out_specs=row(d), out_shape=jax.ShapeDtypeStruct((n, d), F32),
        compiler_params=pltpu.CompilerParams(dimension_semantics=("arbitrary",),
                                             vmem_limit_bytes=VMEM_LIMIT),
        name="outproj",
    )(ao, bo, x2, w_out, final_w.reshape(1, d))


def kernel(x, norm_w, w_in, w_out, lambda_q1, lambda_k1, lambda_q2, lambda_k2, subln_w, rel_bias,
           final_norm_w):
    b, s, d = x.shape
    depth = norm_w.shape[0]
    assert depth == 1, "single-layer trunk"
    assert s % KV_TILE == 0 and (b * s) % PROJ_TM == 0
    layer = 0
    lambda_init = 0.8 - 0.6 * math.exp(-0.3 * layer)
    k_sel = min(TOPK_MAX, s // 4)

    w = w_in[layer]
    main = 8 * HEAD_W
    iq_w = w[:, main:main + IDX_HEADS * IDX_DIM]
    ik_w = w[:, main + IDX_HEADS * IDX_DIM:main + IDX_HEADS * IDX_DIM + IDX_DIM]
    iw_w = w[:, main + IDX_HEADS * IDX_DIM + IDX_DIM:]
    pad = jnp.zeros((d, LANES - IDX_HEADS), w.dtype)
    w_pad = jnp.concatenate([w[:, :main], iq_w, ik_w, ik_w, iw_w, pad], axis=1).astype(BF16)

    x2 = x.reshape(b * s, d)
    aq, ak, av, ag, bq, bk, bv, bg, iq, ik, iw = _inproj(x2, norm_w[layer], w_pad)
    r3 = lambda a: a.reshape(b, s, a.shape[-1])
    bias = _bias_tiles(rel_bias)

    vec = lambda a: a.reshape(1, -1)
    ao = _diff_attention(r3(aq), r3(ak), r3(av), r3(ag), bias[:A_HEADS],
                         vec(lambda_q1[layer]), vec(lambda_k1[layer]),
                         vec(lambda_q2[layer]), vec(lambda_k2[layer]), vec(subln_w[layer]),
                         lambda_init)
    bo = _dsa_attention(r3(iq), r3(ik), r3(iw), r3(bq), r3(bk), r3(bv), r3(bg),
                        bias[A_HEADS:], k_sel)

    out = _outproj(ao.reshape(b * s, -1), bo.reshape(b * s, -1), x2,
                   w_out[layer].astype(BF16), final_norm_w)
    return out.reshape(b, s, d)
```

```python
import functools
import math

import jax
import jax.numpy as jnp
from jax import lax
from jax.experimental import pallas as pl
from jax.experimental.pallas import tpu as pltpu

F32 = jnp.float32
BF16 = jnp.bfloat16
I32 = jnp.int32

EPS = 1e-6
CHUNK = 64
A_HEADS = 4
A_QK_DIM = 64
A_V_DIM = 128
B_HEADS = 8
B_HEAD_DIM = 64
IDX_HEADS = 4
IDX_DIM = 64
TOPK_MAX = 256
N_BUCKETS = 32
HEAD_W = 512
LANES = 128
ROWS = 128
KV_TILE = 256
N_UNITS = 8
PROJ_TM = 512
SCAN_TILE = 512
SCAN_CHAINS = 8
INT_MIN = -2 ** 31
KEY_NEG_INF = INT_MIN + 0x7FFFFF
NEG_INF = float("-inf")
MASKED_LOGIT = -(2.0 ** 120)
VMEM_LIMIT = 56 * 1024 * 1024

_T5_THRESHOLDS = (12, 16, 23, 32, 46, 64, 91)


def _dot_nt(a, b):
    return lax.dot_general(a, b, (((1,), (1,)), ((), ())), preferred_element_type=F32)


def _resident(shape, index_map):
    return pl.BlockSpec(shape, index_map, pipeline_mode=pl.Buffered(1))


def _inproj_kernel(x_ref, nw_ref, w_ref, aq, ak, av, ag, bq, bk, bv, bg, iq, ik, iw,
                   *, q_scale, idx_scale):
    x = x_ref[...]
    ms = jnp.mean(x * x, axis=-1, keepdims=True)
    h = ((x * lax.rsqrt(ms + EPS)) * nw_ref[...]).astype(BF16)

    def proj(c):
        return jnp.dot(h, w_ref[:, c * HEAD_W:(c + 1) * HEAD_W], preferred_element_type=F32)

    aq[...] = (proj(0) * q_scale).astype(BF16)
    ak[...] = proj(1).astype(BF16)
    av[...] = proj(2).astype(BF16)
    ag[...] = proj(3)
    bq[...] = (proj(4) * q_scale).astype(BF16)
    bk[...] = proj(5).astype(BF16)
    bv[...] = proj(6).astype(BF16)
    bg[...] = proj(7)
    tail = proj(8)
    iq[...] = tail[:, 0:256].astype(BF16)
    ik[...] = tail[:, 256:384].astype(BF16)
    iw[...] = tail[:, 384:512] * idx_scale


def _inproj(x2, norm_w, w_pad):
    n, d = x2.shape
    grid = (n // PROJ_TM,)
    row = lambda w: pl.BlockSpec((PROJ_TM, w), lambda i: (i, 0))
    out_shape = ([jax.ShapeDtypeStruct((n, HEAD_W), BF16)] * 3 + [jax.ShapeDtypeStruct((n, HEAD_W), F32)]) * 2 + [
        jax.ShapeDtypeStruct((n, 256), BF16), jax.ShapeDtypeStruct((n, LANES), BF16),
        jax.ShapeDtypeStruct((n, LANES), F32)]
    out_specs = [row(HEAD_W)] * 8 + [row(256), row(LANES), row(LANES)]
    kern = functools.partial(_inproj_kernel, q_scale=A_QK_DIM ** -0.5,
                             idx_scale=(IDX_HEADS ** -0.5) * (IDX_DIM ** -0.5))
    return pl.pallas_call(
        kern, grid=grid,
        in_specs=[row(d), pl.BlockSpec((1, d), lambda i: (0, 0)),
                  pl.BlockSpec(w_pad.shape, lambda i: (0, 0))],
        out_specs=out_specs, out_shape=out_shape,
        compiler_params=pltpu.CompilerParams(dimension_semantics=("arbitrary",),
                                             vmem_limit_bytes=VMEM_LIMIT),
        name="inproj",
    )(x2, norm_w.reshape(1, d), w_pad)


def _bias_kernel(tab_ref, out_ref):
    h = pl.program_id(0)
    t = lax.broadcasted_iota(I32, (KV_TILE, KV_TILE), 0)
    s = lax.broadcasted_iota(I32, (KV_TILE, KV_TILE), 1)
    far = tab_ref[N_BUCKETS // 2 - 1, h]
    for kind in (0, 1):
        rel = s - t - KV_TILE * kind
        n = jnp.abs(rel)
        large = jnp.full_like(n, N_BUCKETS // 4)
        for thr in _T5_THRESHOLDS:
            large = large + (n >= thr).astype(I32)
        bucket = jnp.where(rel > 0, N_BUCKETS // 2, 0) + jnp.where(n < N_BUCKETS // 4, n, large)
        bias = jnp.zeros((KV_TILE, KV_TILE), F32)
        for b in range(N_BUCKETS):
            bias = jnp.where(bucket == b, tab_ref[b, h], bias)
        bias = bias - far
        if kind == 0:
            blocked = jnp.logical_and(h < A_HEADS, (s // CHUNK) > (t // CHUNK))
            bias = jnp.where(blocked, NEG_INF, bias)
        out_ref[kind] = bias


def _bias_tiles(rel_bias):
    nh = rel_bias.shape[1]
    return pl.pallas_call(
        _bias_kernel, grid=(nh,),
        in_specs=[pl.BlockSpec(memory_space=pltpu.SMEM)],
        out_specs=pl.BlockSpec((None, 2, KV_TILE, KV_TILE), lambda h: (h, 0, 0, 0)),
        out_shape=jax.ShapeDtypeStruct((nh, 2, KV_TILE, KV_TILE), F32),
        name="bias_tiles",
    )(rel_bias)


def _stack_masked_queries(q, qm_scr):
    low = lax.broadcasted_iota(I32, (1, LANES), 1) < LANES // 2
    for p in range(N_UNITS // 2):
        qp = q[:, p * LANES:(p + 1) * LANES]
        zero = jnp.zeros_like(qp)
        qm_scr[p, 0:ROWS, 0:LANES] = jnp.where(low, qp, zero)
        qm_scr[p, ROWS:2 * ROWS, 0:LANES] = jnp.where(low, zero, qp)
    if qm_scr.shape[-1] == 2 * LANES:
        hit = (lax.broadcasted_iota(I32, (ROWS, LANES), 0)
               == lax.broadcasted_iota(I32, (ROWS, LANES), 1))
        eye = jnp.where(hit, MASKED_LOGIT, 0.0).astype(BF16)
        for p in range(N_UNITS // 2):
            qm_scr[p, 0:ROWS, LANES:2 * LANES] = eye
            qm_scr[p, ROWS:2 * ROWS, LANES:2 * LANES] = eye


def _two_pass_attention(i, r0, qm_scr, k_ref, v_ref, bias_ref, bias_head, unsel_scr,
                        s_scr, mx_scr, l_scr, acc_scr):
    n_tiles = i + 1
    n_far = jnp.maximum(i - 1, 0)
    mx_scr[...] = jnp.full(mx_scr.shape, NEG_INF, F32)
    l_scr[...] = jnp.zeros(l_scr.shape, F32)
    acc_scr[...] = jnp.zeros(acc_scr.shape, F32)

    def lane_groups(x):
        return [x[:, c * LANES:(c + 1) * LANES] for c in range(x.shape[1] // LANES)]

    def run_tiles(count, fn):
        def body(t, carry):
            fn(8 * t, 8 * KV_TILE)
            return carry
        lax.fori_loop(0, count // 8, body, 0)

        @pl.when(count % 8 >= 4)
        def _():
            fn(8 * (count // 8), 4 * KV_TILE)

        @pl.when(count % 4 >= 2)
        def _():
            fn(4 * (count // 4), 2 * KV_TILE)

        @pl.when(count % 2 == 1)
        def _():
            fn(count - 1, KV_TILE)

    def logits_tile(j, width, bias_kinds=()):
        start = pl.multiple_of(j * KV_TILE, KV_TILE)
        for p in range(N_UNITS // 2):
            kc = k_ref[pl.ds(start, width), p * LANES:(p + 1) * LANES]
            if unsel_scr is not None:
                kc = jnp.concatenate([kc, unsel_scr[pl.ds(start, width), :]], axis=1)
            s2 = _dot_nt(qm_scr[p], kc)
            for half in range(2):
                u = 2 * p + half
                s = s2[half * ROWS:(half + 1) * ROWS]
                if bias_kinds:
                    s = s + jnp.concatenate(
                        [bias_ref[bias_head(u), kind, pl.ds(r0, ROWS), :] for kind in bias_kinds],
                        axis=1)
                s_scr[u, :, pl.ds(start, width)] = s
                mx_scr[u] = jnp.maximum(mx_scr[u], functools.reduce(jnp.maximum, lane_groups(s)))

    run_tiles(n_far, logits_tile)

    @pl.when(i >= 1)
    def _():
        logits_tile(i - 1, 2 * KV_TILE, (1, 0))

    @pl.when(i == 0)
    def _():
        logits_tile(0, KV_TILE, (0,))

    for u in range(N_UNITS):
        m = jnp.max(mx_scr[u], axis=1, keepdims=True)
        mx_scr[u] = jnp.broadcast_to(m, (ROWS, LANES))

    def exp_tile(j, width):
        start = pl.multiple_of(j * KV_TILE, KV_TILE)
        for u in range(N_UNITS):
            m = mx_scr[u]
            ps = [jnp.exp(sg - m) for sg in lane_groups(s_scr[u, :, pl.ds(start, width)])]
            l_scr[u] = l_scr[u] + functools.reduce(lambda a, b: a + b, ps)
            pb = jnp.concatenate(ps, axis=1).astype(BF16)
            g = u // 2
            vc = v_ref[pl.ds(start, width), g * LANES:(g + 1) * LANES]
            acc_scr[u] = acc_scr[u] + jnp.dot(pb, vc, preferred_element_type=F32)

    run_tiles(n_tiles, exp_tile)


def _normalised(u, l_scr, acc_scr):
    return acc_scr[u] / jnp.sum(l_scr[u], axis=1, keepdims=True)


def _tree_sum(x):
    parts = [x[t] for t in range(x.shape[0])]
    while len(parts) > 1:
        parts = [parts[t] + parts[t + 1] for t in range(0, len(parts) - 1, 2)] + (
            [parts[-1]] if len(parts) % 2 else [])
    return parts[0]


def _silu(g):
    return g * (1.0 / (1.0 + jnp.exp(-g)))


def _attention_scratch(s, q_width=LANES):
    return [pltpu.VMEM((N_UNITS // 2, 2 * ROWS, q_width), BF16),
            pltpu.VMEM((N_UNITS, ROWS, s), F32),
            pltpu.VMEM((N_UNITS, ROWS, LANES), F32),
            pltpu.VMEM((N_UNITS, ROWS, LANES), F32),
            pltpu.VMEM((N_UNITS, ROWS, LANES), F32)]


def _diff_kernel(lq1, lk1, lq2, lk2, subw_ref, q_ref, k_ref, v_ref, g_ref, bias_ref, o_ref,
                 qm_scr, s_scr, mx_scr, l_scr, acc_scr, *, lambda_init):
    tb = pl.program_id(1)
    i = tb // 2
    r0 = pl.multiple_of((tb % 2) * ROWS, ROWS)
    lam = (jnp.exp(jnp.sum(lq1[...] * lk1[...], axis=-1, keepdims=True))
           - jnp.exp(jnp.sum(lq2[...] * lk2[...], axis=-1, keepdims=True)) + lambda_init)
    _stack_masked_queries(q_ref[...], qm_scr)
    _two_pass_attention(i, r0, qm_scr, k_ref, v_ref, bias_ref, lambda u: u // 2, None,
                        s_scr, mx_scr, l_scr, acc_scr)
    for h in range(A_HEADS):
        o = _normalised(2 * h, l_scr, acc_scr) - lam * _normalised(2 * h + 1, l_scr, acc_scr)
        y = (o * lax.rsqrt(jnp.mean(o * o, axis=-1, keepdims=True) + EPS)) * subw_ref[...]
        y = y * (1.0 - lambda_init)
        cols = slice(h * LANES, (h + 1) * LANES)
        o_ref[:, cols] = (y * _silu(g_ref[:, cols])).astype(BF16)


def _diff_attention(aq, ak, av, ag, bias, lq1, lk1, lq2, lk2, subln_w, lambda_init):
    b, s, _ = aq.shape
    grid = (b, s // ROWS)
    vec = lambda a: pl.BlockSpec((1, a.shape[-1]), lambda bi, t: (0, 0))
    rows = pl.BlockSpec((None, ROWS, HEAD_W), lambda bi, t: (bi, t, 0))
    full = _resident((None, s, HEAD_W), lambda bi, t: (bi, 0, 0))
    kern = functools.partial(_diff_kernel, lambda_init=lambda_init)
    return pl.pallas_call(
        kern, grid=grid,
        in_specs=[vec(lq1), vec(lk1), vec(lq2), vec(lk2), vec(subln_w), rows, full, full, rows,
                  _resident(bias.shape, lambda bi, t: (0, 0, 0, 0))],
        out_specs=rows,
        out_shape=jax.ShapeDtypeStruct((b, s, HEAD_W), BF16),
        scratch_shapes=_attention_scratch(s),
        compiler_params=pltpu.CompilerParams(dimension_semantics=("arbitrary",) * 2,
                                             vmem_limit_bytes=VMEM_LIMIT),
        name="diff_attention",
    )(lq1, lk1, lq2, lk2, subln_w, aq, ak, av, ag, bias)


def _dsa_kernel(iq_ref, ik_ref, iw_ref, q_ref, k_ref, v_ref, g_ref, bias_ref, o_ref,
                sc_scr, iqm_scr, unsel_scr, qm_scr, s_scr, mx_scr, l_scr, acc_scr, *, k_sel):
    tb = pl.program_id(1)
    i = tb // 2
    n_tiles = i + 1
    n_scan = (n_tiles + 1) // 2
    r0 = pl.multiple_of((tb % 2) * ROWS, ROWS)
    lane = lax.broadcasted_iota(I32, (1, LANES), 1)
    low = lane < B_HEAD_DIM
    q_chunk = (tb * ROWS + lane) // CHUNK
    key_in_tile = lax.broadcasted_iota(I32, (KV_TILE, LANES), 0)

    iq = iq_ref[...]
    for hh in range(IDX_HEADS):
        pair = iq[:, (hh // 2) * LANES:(hh // 2 + 1) * LANES]
        iqm_scr[hh * ROWS:(hh + 1) * ROWS, :] = jnp.where(low == (hh % 2 == 0), pair,
                                                          jnp.zeros_like(pair))
    iw_t = iw_ref[...].T

    def score_tile(j, diagonal):
        start = pl.multiple_of(j * KV_TILE, KV_TILE)
        logits = _dot_nt(ik_ref[pl.ds(start, KV_TILE), :], iqm_scr[...])
        sc = jnp.zeros((KV_TILE, LANES), F32)
        for hh in range(IDX_HEADS):
            sc = sc + iw_t[hh:hh + 1, :] * jnp.maximum(logits[:, hh * ROWS:(hh + 1) * ROWS], 0.0)
        if diagonal:
            sc = jnp.where(((start + key_in_tile) // CHUNK) <= q_chunk, sc, NEG_INF)
        sc_scr[pl.ds(start, KV_TILE), :] = sc

    def far_scores(t, carry):
        for d in range(4):
            score_tile(4 * t + d, False)
        return carry

    lax.fori_loop(0, i // 4, far_scores, 0)

    @pl.when((i // 2) % 2 == 1)
    def _():
        score_tile(4 * (i // 4), False)
        score_tile(4 * (i // 4) + 1, False)

    @pl.when(i % 2 == 1)
    def _():
        score_tile(i - 1, False)
        score_tile(i, True)

    @pl.when(i % 2 == 0)
    def _():
        score_tile(i, True)

    @pl.when(n_tiles % 2 == 1)
    def _():
        pad = pl.ds(pl.multiple_of(n_tiles * KV_TILE, KV_TILE), KV_TILE)
        sc_scr[pad, :] = jnp.full((KV_TILE, LANES), NEG_INF, F32)

    def key_to_float(key):
        bits = key ^ (lax.shift_right_arithmetic(key, 31) & 0x7FFFFFFF)
        return lax.bitcast_convert_type(bits, F32)

    def scan_count(pred):
        def body(t, accs):
            blk = sc_scr[pl.ds(pl.multiple_of(t * SCAN_TILE, SCAN_TILE), SCAN_TILE), :]
            hit = pred(blk.reshape(SCAN_TILE // 8, 8, LANES))
            accs = list(accs)
            for v in range(SCAN_TILE // 8):
                c = v % len(accs)
                accs[c] = jnp.where(hit[v], accs[c] + 1.0, accs[c])
            return tuple(accs)
        zero = jnp.zeros((8, LANES), F32)
        accs = lax.fori_loop(0, n_scan, body, (zero,) * SCAN_CHAINS)
        return jnp.sum(functools.reduce(lambda a, b: a + b, accs), axis=0, keepdims=True)

    kf = float(k_sel)

    def bit_step(b, carry):
        key, c_key = carry
        cand = key + lax.shift_left(jnp.int32(1), 31 - b)
        cand_f = key_to_float(cand)
        c = scan_count(lambda blk: blk >= cand_f)
        ok = jnp.logical_or(c >= kf, cand <= KEY_NEG_INF)
        return jnp.where(ok, cand, key), jnp.where(ok, c, c_key)

    n_keys = jnp.full((1, LANES), 1.0, F32) * (n_tiles * KV_TILE).astype(F32)
    thr_key, c_thr = lax.fori_loop(0, 32, bit_step,
                                   (jnp.full((1, LANES), INT_MIN, I32), n_keys))
    thr = key_to_float(thr_key)

    def tile_count(pred):
        def body(j, acc):
            start = pl.multiple_of(j * KV_TILE, KV_TILE)
            hit = jnp.where(pred(sc_scr[pl.ds(start, KV_TILE), :], start + key_in_tile), 1.0, 0.0)
            return acc + _tree_sum(hit.reshape(KV_TILE // 8, 8, LANES))
        acc = lax.fori_loop(0, n_tiles, body, jnp.zeros((8, LANES), F32))
        return jnp.sum(acc, axis=0, keepdims=True)

    def tie_limit():
        need = kf - tile_count(lambda kc, idx: kc > thr)

        def idx_step(b, lim):
            cand = lim + lax.shift_left(jnp.int32(1), 11 - b)
            f = tile_count(lambda kc, idx: jnp.logical_and(kc == thr, idx < cand))
            return jnp.where(f < need, cand, lim)

        return lax.fori_loop(0, 12, idx_step, jnp.zeros((1, LANES), I32))

    lim = lax.cond(jnp.max(c_thr) > kf, tie_limit,
                   lambda: jnp.full((1, LANES), 2 ** 30, I32))

    def mask_tile(j, diagonal):
        start = pl.multiple_of(j * KV_TILE, KV_TILE)
        kc = sc_scr[pl.ds(start, KV_TILE), :]
        idx = start + key_in_tile
        sel = jnp.logical_or(kc > thr, jnp.logical_and(kc == thr, idx <= lim))
        if diagonal:
            sel = jnp.logical_and(sel, (idx // CHUNK) <= q_chunk)
        unsel_scr[pl.ds(start, KV_TILE), :] = jnp.where(sel, 0.0, 1.0).astype(BF16)

    def far_mask(j, carry):
        mask_tile(j, False)
        return carry

    lax.fori_loop(0, i, far_mask, 0)
    mask_tile(i, True)

    _stack_masked_queries(q_ref[...], qm_scr)
    _two_pass_attention(i, r0, qm_scr, k_ref, v_ref, bias_ref, lambda u: u, unsel_scr,
                        s_scr, mx_scr, l_scr, acc_scr)
    for p in range(B_HEADS // 2):
        o = jnp.where(low, _normalised(2 * p, l_scr, acc_scr), _normalised(2 * p + 1, l_scr, acc_scr))
        cols = slice(p * LANES, (p + 1) * LANES)
        o_ref[:, cols] = (o * _silu(g_ref[:, cols])).astype(BF16)


def _dsa_attention(iq, ik, iw, bq, bk, bv, bg, bias, k_sel):
    b, s, _ = bq.shape
    grid = (b, s // ROWS)
    rows = lambda w: pl.BlockSpec((None, ROWS, w), lambda bi, t: (bi, t, 0))
    full = lambda w: _resident((None, s, w), lambda bi, t: (bi, 0, 0))
    kern = functools.partial(_dsa_kernel, k_sel=k_sel)
    return pl.pallas_call(
        kern, grid=grid,
        in_specs=[rows(256), full(LANES), rows(LANES), rows(HEAD_W), full(HEAD_W), full(HEAD_W),
                  rows(HEAD_W), _resident(bias.shape, lambda bi, t: (0, 0, 0, 0))],
        out_specs=rows(HEAD_W),
        out_shape=jax.ShapeDtypeStruct((b, s, HEAD_W), BF16),
        scratch_shapes=[pltpu.VMEM((s, LANES), F32),
                        pltpu.VMEM((IDX_HEADS * ROWS, LANES), BF16),
                        pltpu.VMEM((s, LANES), BF16)]
        + _attention_scratch(s, 2 * LANES),
        compiler_params=pltpu.CompilerParams(dimension_semantics=("arbitrary",) * 2,
                                             vmem_limit_bytes=VMEM_LIMIT),
        name="dsa_attention",
    )(iq, ik, iw, bq, bk, bv, bg, bias)


def _outproj_kernel(ao_ref, bo_ref, x_ref, w_ref, fw_ref, o_ref):
    y = jnp.dot(ao_ref[...], w_ref[0:HEAD_W, :], preferred_element_type=F32)
    y = y + jnp.dot(bo_ref[...], w_ref[HEAD_W:2 * HEAD_W, :], preferred_element_type=F32)
    z = x_ref[...] + y
    o_ref[...] = (z * lax.rsqrt(jnp.mean(z * z, axis=-1, keepdims=True) + EPS)) * fw_ref[...]


def _outproj(ao, bo, x2, w_out, final_w):
    n, d = x2.shape
    row = lambda w: pl.BlockSpec((PROJ_TM, w), lambda i: (i, 0))
    return pl.pallas_call(
        _outproj_kernel, grid=(n // PROJ_TM,),
        in_specs=[row(HEAD_W), row(HEAD_W), row(d), pl.BlockSpec(w_out.shape, lambda i: (0, 0)),
                  pl.BlockSpec((1, d), lambda i: (0, 0))],
        out_specs=row(d), out_shape=jax.ShapeDtypeStruct((n, d), F32),
        compiler_params=pltpu.CompilerParams(dimension_semantics=("arbitrary",),
                                             vmem_limit_bytes=VMEM_LIMIT),
        name="outproj",
    )(ao, bo, x2, w_out, final_w.reshape(1, d))


def kernel(x, norm_w, w_in, w_out, lambda_q1, lambda_k1, lambda_q2, lambda_k2, subln_w, rel_bias,
           final_norm_w):
    b, s, d = x.shape
    depth = norm_w.shape[0]
    assert depth == 1, "single-layer trunk"
    assert s % KV_TILE == 0 and (b * s) % PROJ_TM == 0
    layer = 0
    lambda_init = 0.8 - 0.6 * math.exp(-0.3 * layer)
    k_sel = min(TOPK_MAX, s // 4)

    w = w_in[layer]
    main = 8 * HEAD_W
    iq_w = w[:, main:main + IDX_HEADS * IDX_DIM]
    ik_w = w[:, main + IDX_HEADS * IDX_DIM:main + IDX_HEADS * IDX_DIM + IDX_DIM]
    iw_w = w[:, main + IDX_HEADS * IDX_DIM + IDX_DIM:]
    pad = jnp.zeros((d, LANES - IDX_HEADS), w.dtype)
    w_pad = jnp.concatenate([w[:, :main], iq_w, ik_w, ik_w, iw_w, pad], axis=1).astype(BF16)

    x2 = x.reshape(b * s, d)
    aq, ak, av, ag, bq, bk, bv, bg, iq, ik, iw = _inproj(x2, norm_w[layer], w_pad)
    r3 = lambda a: a.reshape(b, s, a.shape[-1])
    bias = _bias_tiles(rel_bias)

    vec = lambda a: a.reshape(1, -1)
    ao = _diff_attention(r3(aq), r3(ak), r3(av), r3(ag), bias[:A_HEADS],
                         vec(lambda_q1[layer]), vec(lambda_k1[layer]),
                         vec(lambda_q2[layer]), vec(lambda_k2[layer]), vec(subln_w[layer]),
                         lambda_init)
    bo = _dsa_attention(r3(iq), r3(ik), r3(iw), r3(bq), r3(bk), r3(bv), r3(bg),
                        bias[A_HEADS:], k_sel)

    out = _outproj(ao.reshape(b * s, -1), bo.reshape(b * s, -1), x2,
                   w_out[layer].astype(BF16), final_norm_w)
    return out.reshape(b, s, d)
```

```python
import functools
import math

import jax
import jax.numpy as jnp
from jax import lax
from jax.experimental import pallas as pl
from jax.experimental.pallas import tpu as pltpu

F32 = jnp.float32
BF16 = jnp.bfloat16
I32 = jnp.int32

EPS = 1e-6
CHUNK = 64
A_HEADS = 4
A_QK_DIM = 64
A_V_DIM = 128
B_HEADS = 8
B_HEAD_DIM = 64
IDX_HEADS = 4
IDX_DIM = 64
TOPK_MAX = 256
N_BUCKETS = 32
HEAD_W = 512
LANES = 128
ROWS = 128
KV_TILE = 256
N_UNITS = 8
PROJ_TM = 512
SCAN_TILE = 512
SCAN_CHAINS = 8
INT_MIN = -2 ** 31
KEY_NEG_INF = INT_MIN + 0x7FFFFF
NEG_INF = float("-inf")
MASKED_LOGIT = -(2.0 ** 120)
VMEM_LIMIT = 56 * 1024 * 1024

_T5_THRESHOLDS = (12, 16, 23, 32, 46, 64, 91)


def _dot_nt(a, b):
    return lax.dot_general(a, b, (((1,), (1,)), ((), ())), preferred_element_type=F32)


def _resident(shape, index_map):
    return pl.BlockSpec(shape, index_map, pipeline_mode=pl.Buffered(1))


def _inproj_kernel(x_ref, nw_ref, w_ref, aq, ak, av, ag, bq, bk, bv, bg, iq, ik, iw,
                   *, q_scale, idx_scale):
    x = x_ref[...]
    ms = jnp.mean(x * x, axis=-1, keepdims=True)
    h = ((x * lax.rsqrt(ms + EPS)) * nw_ref[...]).astype(BF16)

    def proj(c):
        return jnp.dot(h, w_ref[:, c * HEAD_W:(c + 1) * HEAD_W], preferred_element_type=F32)

    aq[...] = (proj(0) * q_scale).astype(BF16)
    ak[...] = proj(1).astype(BF16)
    av[...] = proj(2).astype(BF16)
    ag[...] = proj(3)
    bq[...] = (proj(4) * q_scale).astype(BF16)
    bk[...] = proj(5).astype(BF16)
    bv[...] = proj(6).astype(BF16)
    bg[...] = proj(7)
    tail = proj(8)
    iq[...] = tail[:, 0:256].astype(BF16)
    ik[...] = tail[:, 256:384].astype(BF16)
    iw[...] = tail[:, 384:512] * idx_scale


def _inproj(x2, norm_w, w_pad):
    n, d = x2.shape
    grid = (n // PROJ_TM,)
    row = lambda w: pl.BlockSpec((PROJ_TM, w), lambda i: (i, 0))
    out_shape = ([jax.ShapeDtypeStruct((n, HEAD_W), BF16)] * 3 + [jax.ShapeDtypeStruct((n, HEAD_W), F32)]) * 2 + [
        jax.ShapeDtypeStruct((n, 256), BF16), jax.ShapeDtypeStruct((n, LANES), BF16),
        jax.ShapeDtypeStruct((n, LANES), F32)]
    out_specs = [row(HEAD_W)] * 8 + [row(256), row(LANES), row(LANES)]
    kern = functools.partial(_inproj_kernel, q_scale=A_QK_DIM ** -0.5,
                             idx_scale=(IDX_HEADS ** -0.5) * (IDX_DIM ** -0.5))
    return pl.pallas_call(
        kern, grid=grid,
        in_specs=[row(d), pl.BlockSpec((1, d), lambda i: (0, 0)),
                  pl.BlockSpec(w_pad.shape, lambda i: (0, 0))],
        out_specs=out_specs, out_shape=out_shape,
        compiler_params=pltpu.CompilerParams(dimension_semantics=("arbitrary",),
                                             vmem_limit_bytes=VMEM_LIMIT),
        name="inproj",
    )(x2, norm_w.reshape(1, d), w_pad)


def _bias_kernel(tab_ref, out_ref):
    h = pl.program_id(0)
    t = lax.broadcasted_iota(I32, (KV_TILE, KV_TILE), 0)
    s = lax.broadcasted_iota(I32, (KV_TILE, KV_TILE), 1)
    far = tab_ref[N_BUCKETS // 2 - 1, h]
    for kind in (0, 1):
        rel = s - t - KV_TILE * kind
        n = jnp.abs(rel)
        large = jnp.full_like(n, N_BUCKETS // 4)
        for thr in _T5_THRESHOLDS:
            large = large + (n >= thr).astype(I32)
        bucket = jnp.where(rel > 0, N_BUCKETS // 2, 0) + jnp.where(n < N_BUCKETS // 4, n, large)
        bias = jnp.zeros((KV_TILE, KV_TILE), F32)
        for b in range(N_BUCKETS):
            bias = jnp.where(bucket == b, tab_ref[b, h], bias)
        bias = bias - far
        if kind == 0:
            blocked = jnp.logical_and(h < A_HEADS, (s // CHUNK) > (t // CHUNK))
            bias = jnp.where(blocked, NEG_INF, bias)
        out_ref[kind] = bias


def _bias_tiles(rel_bias):
    nh = rel_bias.shape[1]
    return pl.pallas_call(
        _bias_kernel, grid=(nh,),
        in_specs=[pl.BlockSpec(memory_space=pltpu.SMEM)],
        out_specs=pl.BlockSpec((None, 2, KV_TILE, KV_TILE), lambda h: (h, 0, 0, 0)),
        out_shape=jax.ShapeDtypeStruct((nh, 2, KV_TILE, KV_TILE), F32),
        name="bias_tiles",
    )(rel_bias)


def _stack_masked_queries(q, qm_scr):
    low = lax.broadcasted_iota(I32, (1, LANES), 1) < LANES // 2
    for p in range(N_UNITS // 2):
        qp = q[:, p * LANES:(p + 1) * LANES]
        zero = jnp.zeros_like(qp)
        qm_scr[p, 0:ROWS, 0:LANES] = jnp.where(low, qp, zero)
        qm_scr[p, ROWS:2 * ROWS, 0:LANES] = jnp.where(low, zero, qp)
    if qm_scr.shape[-1] == 2 * LANES:
        hit = (lax.broadcasted_iota(I32, (ROWS, LANES), 0)
               == lax.broadcasted_iota(I32, (ROWS, LANES), 1))
        eye = jnp.where(hit, MASKED_LOGIT, 0.0).astype(BF16)
        for p in range(N_UNITS // 2):
            qm_scr[p, 0:ROWS, LANES:2 * LANES] = eye
            qm_scr[p, ROWS:2 * ROWS, LANES:2 * LANES] = eye


def _two_pass_attention(i, r0, qm_scr, k_ref, v_ref, bias_ref, bias_head, unsel_scr,
                        s_scr, mx_scr, l_scr, acc_scr):
    n_tiles = i + 1
    n_far = jnp.maximum(i - 1, 0)
    mx_scr[...] = jnp.full(mx_scr.shape, NEG_INF, F32)
    l_scr[...] = jnp.zeros(l_scr.shape, F32)
    acc_scr[...] = jnp.zeros(acc_scr.shape, F32)

    def lane_groups(x):
        return [x[:, c * LANES:(c + 1) * LANES] for c in range(x.shape[1] // LANES)]

    def run_tiles(count, fn):
        def body(t, carry):
            fn(8 * t, 8 * KV_TILE)
            return carry
        lax.fori_loop(0, count // 8, body, 0)

        @pl.when(count % 8 >= 4)
        def _():
            fn(8 * (count // 8), 4 * KV_TILE)

        @pl.when(count % 4 >= 2)
        def _():
            fn(4 * (count // 4), 2 * KV_TILE)

        @pl.when(count % 2 == 1)
        def _():
            fn(count - 1, KV_TILE)

    def logits_tile(j, width, bias_kinds=()):
        start = pl.multiple_of(j * KV_TILE, KV_TILE)
        for p in range(N_UNITS // 2):
            kc = k_ref[pl.ds(start, width), p * LANES:(p + 1) * LANES]
            if unsel_scr is not None:
                kc = jnp.concatenate([kc, unsel_scr[pl.ds(start, width), :]], axis=1)
            s2 = _dot_nt(qm_scr[p], kc)
            for half in range(2):
                u = 2 * p + half
                s = s2[half * ROWS:(half + 1) * ROWS]
                if bias_kinds:
                    s = s + jnp.concatenate(
                        [bias_ref[bias_head(u), kind, pl.ds(r0, ROWS), :] for kind in bias_kinds],
                        axis=1)
                s_scr[u, :, pl.ds(start, width)] = s
                mx_scr[u] = jnp.maximum(mx_scr[u], functools.reduce(jnp.maximum, lane_groups(s)))

    run_tiles(n_far, logits_tile)

    @pl.when(i >= 1)
    def _():
        logits_tile(i - 1, 2 * KV_TILE, (1, 0))

    @pl.when(i == 0)
    def _():
        logits_tile(0, KV_TILE, (0,))

    for u in range(N_UNITS):
        m = jnp.max(mx_scr[u], axis=1, keepdims=True)
        mx_scr[u] = jnp.broadcast_to(m, (ROWS, LANES))

    def exp_tile(j, width):
        start = pl.multiple_of(j * KV_TILE, KV_TILE)
        for u in range(N_UNITS):
            m = mx_scr[u]
            ps = [jnp.exp(sg - m) for sg in lane_groups(s_scr[u, :, pl.ds(start, width)])]
            pb = jnp.concatenate(ps, axis=1).astype(BF16)
            g = u // 2
            vc = v_ref[pl.ds(start, width), g * LANES:(g + 1) * LANES]
            both = jnp.dot(pb, jnp.concatenate([vc, jnp.ones_like(vc)], axis=1),
                           preferred_element_type=F32)
            acc_scr[u] = acc_scr[u] + both[:, :LANES]
            l_scr[u] = l_scr[u] + both[:, LANES:]

    run_tiles(n_tiles, exp_tile)


def _normalised(u, l_scr, acc_scr):
    return acc_scr[u] / l_scr[u]


def _tree_sum(x):
    parts = [x[t] for t in range(x.shape[0])]
    while len(parts) > 1:
        parts = [parts[t] + parts[t + 1] for t in range(0, len(parts) - 1, 2)] + (
            [parts[-1]] if len(parts) % 2 else [])
    return parts[0]


def _silu(g):
    return g * (1.0 / (1.0 + jnp.exp(-g)))


def _attention_scratch(s, q_width=LANES):
    return [pltpu.VMEM((N_UNITS // 2, 2 * ROWS, q_width), BF16),
            pltpu.VMEM((N_UNITS, ROWS, s), F32),
            pltpu.VMEM((N_UNITS, ROWS, LANES), F32),
            pltpu.VMEM((N_UNITS, ROWS, LANES), F32),
            pltpu.VMEM((N_UNITS, ROWS, LANES), F32)]


def _diff_kernel(lq1, lk1, lq2, lk2, subw_ref, q_ref, k_ref, v_ref, g_ref, bias_ref, o_ref,
                 qm_scr, s_scr, mx_scr, l_scr, acc_scr, *, lambda_init):
    tb = pl.program_id(1)
    i = tb // 2
    r0 = pl.multiple_of((tb % 2) * ROWS, ROWS)
    lam = (jnp.exp(jnp.sum(lq1[...] * lk1[...], axis=-1, keepdims=True))
           - jnp.exp(jnp.sum(lq2[...] * lk2[...], axis=-1, keepdims=True)) + lambda_init)
    _stack_masked_queries(q_ref[...], qm_scr)
    _two_pass_attention(i, r0, qm_scr, k_ref, v_ref, bias_ref, lambda u: u // 2, None,
                        s_scr, mx_scr, l_scr, acc_scr)
    for h in range(A_HEADS):
        o = _normalised(2 * h, l_scr, acc_scr) - lam * _normalised(2 * h + 1, l_scr, acc_scr)
        y = (o * lax.rsqrt(jnp.mean(o * o, axis=-1, keepdims=True) + EPS)) * subw_ref[...]
        y = y * (1.0 - lambda_init)
        cols = slice(h * LANES, (h + 1) * LANES)
        o_ref[:, cols] = (y * _silu(g_ref[:, cols])).astype(BF16)


def _diff_attention(aq, ak, av, ag, bias, lq1, lk1, lq2, lk2, subln_w, lambda_init):
    b, s, _ = aq.shape
    grid = (b, s // ROWS)
    vec = lambda a: pl.BlockSpec((1, a.shape[-1]), lambda bi, t: (0, 0))
    rows = pl.BlockSpec((None, ROWS, HEAD_W), lambda bi, t: (bi, t, 0))
    full = _resident((None, s, HEAD_W), lambda bi, t: (bi, 0, 0))
    kern = functools.partial(_diff_kernel, lambda_init=lambda_init)
    return pl.pallas_call(
        kern, grid=grid,
        in_specs=[vec(lq1), vec(lk1), vec(lq2), vec(lk2), vec(subln_w), rows, full, full, rows,
                  _resident(bias.shape, lambda bi, t: (0, 0, 0, 0))],
        out_specs=rows,
        out_shape=jax.ShapeDtypeStruct((b, s, HEAD_W), BF16),
        scratch_shapes=_attention_scratch(s),
        compiler_params=pltpu.CompilerParams(dimension_semantics=("arbitrary",) * 2,
                                             vmem_limit_bytes=VMEM_LIMIT),
        name="diff_attention",
    )(lq1, lk1, lq2, lk2, subln_w, aq, ak, av, ag, bias)


def _dsa_kernel(iq_ref, ik_ref, iw_ref, q_ref, k_ref, v_ref, g_ref, bias_ref, o_ref,
                sc_scr, iqm_scr, unsel_scr, qm_scr, s_scr, mx_scr, l_scr, acc_scr, *, k_sel):
    tb = pl.program_id(1)
    i = tb // 2
    n_tiles = i + 1
    n_scan = (n_tiles + 1) // 2
    r0 = pl.multiple_of((tb % 2) * ROWS, ROWS)
    lane = lax.broadcasted_iota(I32, (1, LANES), 1)
    low = lane < B_HEAD_DIM
    q_chunk = (tb * ROWS + lane) // CHUNK
    key_in_tile = lax.broadcasted_iota(I32, (KV_TILE, LANES), 0)

    iq = iq_ref[...]
    for hh in range(IDX_HEADS):
        pair = iq[:, (hh // 2) * LANES:(hh // 2 + 1) * LANES]
        iqm_scr[hh * ROWS:(hh + 1) * ROWS, :] = jnp.where(low == (hh % 2 == 0), pair,
                                                          jnp.zeros_like(pair))
    iw_t = iw_ref[...].T

    def score_tile(j, diagonal):
        start = pl.multiple_of(j * KV_TILE, KV_TILE)
        logits = _dot_nt(ik_ref[pl.ds(start, KV_TILE), :], iqm_scr[...])
        sc = jnp.zeros((KV_TILE, LANES), F32)
        for hh in range(IDX_HEADS):
            sc = sc + iw_t[hh:hh + 1, :] * jnp.maximum(logits[:, hh * ROWS:(hh + 1) * ROWS], 0.0)
        if diagonal:
            sc = jnp.where(((start + key_in_tile) // CHUNK) <= q_chunk, sc, NEG_INF)
        sc_scr[pl.ds(start, KV_TILE), :] = sc

    def far_scores(t, carry):
        for d in range(4):
            score_tile(4 * t + d, False)
        return carry

    lax.fori_loop(0, i // 4, far_scores, 0)

    @pl.when((i // 2) % 2 == 1)
    def _():
        score_tile(4 * (i // 4), False)
        score_tile(4 * (i // 4) + 1, False)

    @pl.when(i % 2 == 1)
    def _():
        score_tile(i - 1, False)
        score_tile(i, True)

    @pl.when(i % 2 == 0)
    def _():
        score_tile(i, True)

    @pl.when(n_tiles % 2 == 1)
    def _():
        pad = pl.ds(pl.multiple_of(n_tiles * KV_TILE, KV_TILE), KV_TILE)
        sc_scr[pad, :] = jnp.full((KV_TILE, LANES), NEG_INF, F32)

    def key_to_float(key):
        bits = key ^ (lax.shift_right_arithmetic(key, 31) & 0x7FFFFFFF)
        return lax.bitcast_convert_type(bits, F32)

    def scan_count(pred):
        def body(t, accs):
            blk = sc_scr[pl.ds(pl.multiple_of(t * SCAN_TILE, SCAN_TILE), SCAN_TILE), :]
            hit = pred(blk.reshape(SCAN_TILE // 8, 8, LANES))
            accs = list(accs)
            for v in range(SCAN_TILE // 8):
                c = v % len(accs)
                accs[c] = jnp.where(hit[v], accs[c] + 1.0, accs[c])
            return tuple(accs)
        zero = jnp.zeros((8, LANES), F32)
        accs = lax.fori_loop(0, n_scan, body, (zero,) * SCAN_CHAINS)
        return jnp.sum(functools.reduce(lambda a, b: a + b, accs), axis=0, keepdims=True)

    kf = float(k_sel)

    def bit_step(b, carry):
        key, c_key = carry
        cand = key + lax.shift_left(jnp.int32(1), 31 - b)
        cand_f = key_to_float(cand)
        c = scan_count(lambda blk: blk >= cand_f)
        ok = jnp.logical_or(c >= kf, cand <= KEY_NEG_INF)
        return jnp.where(ok, cand, key), jnp.where(ok, c, c_key)

    n_keys = jnp.full((1, LANES), 1.0, F32) * (n_tiles * KV_TILE).astype(F32)
    thr_key, c_thr = lax.fori_loop(0, 32, bit_step,
                                   (jnp.full((1, LANES), INT_MIN, I32), n_keys))
    thr = key_to_float(thr_key)

    def tile_count(pred):
        def body(j, acc):
            start = pl.multiple_of(j * KV_TILE, KV_TILE)
            hit = jnp.where(pred(sc_scr[pl.ds(start, KV_TILE), :], start + key_in_tile), 1.0, 0.0)
            return acc + _tree_sum(hit.reshape(KV_TILE // 8, 8, LANES))
        acc = lax.fori_loop(0, n_tiles, body, jnp.zeros((8, LANES), F32))
        return jnp.sum(acc, axis=0, keepdims=True)

    def tie_limit():
        need = kf - tile_count(lambda kc, idx: kc > thr)

        def idx_step(b, lim):
            cand = lim + lax.shift_left(jnp.int32(1), 11 - b)
            f = tile_count(lambda kc, idx: jnp.logical_and(kc == thr, idx < cand))
            return jnp.where(f < need, cand, lim)

        return lax.fori_loop(0, 12, idx_step, jnp.zeros((1, LANES), I32))

    lim = lax.cond(jnp.max(c_thr) > kf, tie_limit,
                   lambda: jnp.full((1, LANES), 2 ** 30, I32))

    def mask_tile(j, diagonal):
        start = pl.multiple_of(j * KV_TILE, KV_TILE)
        kc = sc_scr[pl.ds(start, KV_TILE), :]
        idx = start + key_in_tile
        sel = jnp.logical_or(kc > thr, jnp.logical_and(kc == thr, idx <= lim))
        if diagonal:
            sel = jnp.logical_and(sel, (idx // CHUNK) <= q_chunk)
        unsel_scr[pl.ds(start, KV_TILE), :] = jnp.where(sel, 0.0, 1.0).astype(BF16)

    def far_mask(j, carry):
        mask_tile(j, False)
        return carry

    lax.fori_loop(0, i, far_mask, 0)
    mask_tile(i, True)

    _stack_masked_queries(q_ref[...], qm_scr)
    _two_pass_attention(i, r0, qm_scr, k_ref, v_ref, bias_ref, lambda u: u, unsel_scr,
                        s_scr, mx_scr, l_scr, acc_scr)
    for p in range(B_HEADS // 2):
        o = jnp.where(low, _normalised(2 * p, l_scr, acc_scr), _normalised(2 * p + 1, l_scr, acc_scr))
        cols = slice(p * LANES, (p + 1) * LANES)
        o_ref[:, cols] = (o * _silu(g_ref[:, cols])).astype(BF16)


def _dsa_attention(iq, ik, iw, bq, bk, bv, bg, bias, k_sel):
    b, s, _ = bq.shape
    grid = (b, s // ROWS)
    rows = lambda w: pl.BlockSpec((None, ROWS, w), lambda bi, t: (bi, t, 0))
    full = lambda w: _resident((None, s, w), lambda bi, t: (bi, 0, 0))
    kern = functools.partial(_dsa_kernel, k_sel=k_sel)
    return pl.pallas_call(
        kern, grid=grid,
        in_specs=[rows(256), full(LANES), rows(LANES), rows(HEAD_W), full(HEAD_W), full(HEAD_W),
                  rows(HEAD_W), _resident(bias.shape, lambda bi, t: (0, 0, 0, 0))],
        out_specs=rows(HEAD_W),
        out_shape=jax.ShapeDtypeStruct((b, s, HEAD_W), BF16),
        scratch_shapes=[pltpu.VMEM((s, LANES), F32),
                        pltpu.VMEM((IDX_HEADS * ROWS, LANES), BF16),
                        pltpu.VMEM((s, LANES), BF16)]
        + _attention_scratch(s, 2 * LANES),
        compiler_params=pltpu.CompilerParams(dimension_semantics=("arbitrary",) * 2,
                                             vmem_limit_bytes=VMEM_LIMIT),
        name="dsa_attention",
    )(iq, ik, iw, bq, bk, bv, bg, bias)


def _outproj_kernel(ao_ref, bo_ref, x_ref, w_ref, fw_ref, o_ref):
    y = jnp.dot(ao_ref[...], w_ref[0:HEAD_W, :], preferred_element_type=F32)
    y = y + jnp.dot(bo_ref[...], w_ref[HEAD_W:2 * HEAD_W, :], preferred_element_type=F32)
    z = x_ref[...] + y
    o_ref[...] = (z * lax.rsqrt(jnp.mean(z * z, axis=-1, keepdims=True) + EPS)) * fw_ref[...]


def _outproj(ao, bo, x2, w_out, final_w):
    n, d = x2.shape
    row = lambda w: pl.BlockSpec((PROJ_TM, w), lambda i: (i, 0))
    return pl.pallas_call(
        _outproj_kernel, grid=(n // PROJ_TM,),
        in_specs=[row(HEAD_W), row(HEAD_W), row(d), pl.BlockSpec(w_out.shape, lambda i: (0, 0)),
                  pl.BlockSpec((1, d), lambda i: (0, 0))],
        out_specs=row(d), out_shape=jax.ShapeDtypeStruct((n, d), F32),
        compiler_params=pltpu.CompilerParams(dimension_semantics=("arbitrary",),
                                             vmem_limit_bytes=VMEM_LIMIT),
        name="outproj",
    )(ao, bo, x2, w_out, final_w.reshape(1, d))


def kernel(x, norm_w, w_in, w_out, lambda_q1, lambda_k1, lambda_q2, lambda_k2, subln_w, rel_bias,
           final_norm_w):
    b, s, d = x.shape
    depth = norm_w.shape[0]
    assert depth == 1, "single-layer trunk"
    assert s % KV_TILE == 0 and (b * s) % PROJ_TM == 0
    layer = 0
    lambda_init = 0.8 - 0.6 * math.exp(-0.3 * layer)
    k_sel = min(TOPK_MAX, s // 4)

    w = w_in[layer]
    main = 8 * HEAD_W
    iq_w = w[:, main:main + IDX_HEADS * IDX_DIM]
    ik_w = w[:, main + IDX_HEADS * IDX_DIM:main + IDX_HEADS * IDX_DIM + IDX_DIM]
    iw_w = w[:, main + IDX_HEADS * IDX_DIM + IDX_DIM:]
    pad = jnp.zeros((d, LANES - IDX_HEADS), w.dtype)
    w_pad = jnp.concatenate([w[:, :main], iq_w, ik_w, ik_w, iw_w, pad], axis=1).astype(BF16)

    x2 = x.reshape(b * s, d)
    aq, ak, av, ag, bq, bk, bv, bg, iq, ik, iw = _inproj(x2, norm_w[layer], w_pad)
    r3 = lambda a: a.reshape(b, s, a.shape[-1])
    bias = _bias_tiles(rel_bias)

    vec = lambda a: a.reshape(1, -1)
    ao = _diff_attention(r3(aq), r3(ak), r3(av), r3(ag), bias[:A_HEADS],
                         vec(lambda_q1[layer]), vec(lambda_k1[layer]),
                         vec(lambda_q2[layer]), vec(lambda_k2[layer]), vec(subln_w[layer]),
                         lambda_init)
    bo = _dsa_attention(r3(iq), r3(ik), r3(iw), r3(bq), r3(bk), r3(bv), r3(bg),
                        bias[A_HEADS:], k_sel)

    out = _outproj(ao.reshape(b * s, -1), bo.reshape(b * s, -1), x2,
                   w_out[layer].astype(BF16), final_norm_w)
    return out.reshape(b, s, d)
```

```python
import functools
import math

import jax
import jax.numpy as jnp
from jax import lax
from jax.experimental import pallas as pl
from jax.experimental.pallas import tpu as pltpu

F32 = jnp.float32
BF16 = jnp.bfloat16
I32 = jnp.int32

EPS = 1e-6
CHUNK = 64
A_HEADS = 4
A_QK_DIM = 64
A_V_DIM = 128
B_HEADS = 8
B_HEAD_DIM = 64
IDX_HEADS = 4
IDX_DIM = 64
TOPK_MAX = 256
N_BUCKETS = 32
HEAD_W = 512
LANES = 128
ROWS = 128
KV_TILE = 256
N_UNITS = 8
PROJ_TM = 512
SCAN_TILE = 512
SCAN_CHAINS = 8
INT_MIN = -2 ** 31
KEY_NEG_INF = INT_MIN + 0x7FFFFF
NEG_INF = float("-inf")
MASKED_LOGIT = -(2.0 ** 120)
VMEM_LIMIT = 56 * 1024 * 1024

_T5_THRESHOLDS = (12, 16, 23, 32, 46, 64, 91)


def _dot_nt(a, b):
    return lax.dot_general(a, b, (((1,), (1,)), ((), ())), preferred_element_type=F32)


def _resident(shape, index_map):
    return pl.BlockSpec(shape, index_map, pipeline_mode=pl.Buffered(1))


def _inproj_kernel(x_ref, nw_ref, w_ref, aq, ak, av, ag, bq, bk, bv, bg, iq, ik, iw,
                   *, q_scale, idx_scale):
    x = x_ref[...]
    ms = jnp.mean(x * x, axis=-1, keepdims=True)
    h = ((x * lax.rsqrt(ms + EPS)) * nw_ref[...]).astype(BF16)

    def proj(c):
        return jnp.dot(h, w_ref[:, c * HEAD_W:(c + 1) * HEAD_W], preferred_element_type=F32)

    aq[...] = (proj(0) * q_scale).astype(BF16)
    ak[...] = proj(1).astype(BF16)
    av[...] = proj(2).astype(BF16)
    ag[...] = proj(3)
    bq[...] = (proj(4) * q_scale).astype(BF16)
    bk[...] = proj(5).astype(BF16)
    bv[...] = proj(6).astype(BF16)
    bg[...] = proj(7)
    tail = proj(8)
    iq[...] = tail[:, 0:256].astype(BF16)
    ik[...] = tail[:, 256:384].astype(BF16)
    iw[...] = tail[:, 384:512] * idx_scale


def _inproj(x2, norm_w, w_pad):
    n, d = x2.shape
    grid = (n // PROJ_TM,)
    row = lambda w: pl.BlockSpec((PROJ_TM, w), lambda i: (i, 0))
    out_shape = ([jax.ShapeDtypeStruct((n, HEAD_W), BF16)] * 3 + [jax.ShapeDtypeStruct((n, HEAD_W), F32)]) * 2 + [
        jax.ShapeDtypeStruct((n, 256), BF16), jax.ShapeDtypeStruct((n, LANES), BF16),
        jax.ShapeDtypeStruct((n, LANES), F32)]
    out_specs = [row(HEAD_W)] * 8 + [row(256), row(LANES), row(LANES)]
    kern = functools.partial(_inproj_kernel, q_scale=A_QK_DIM ** -0.5,
                             idx_scale=(IDX_HEADS ** -0.5) * (IDX_DIM ** -0.5))
    return pl.pallas_call(
        kern, grid=grid,
        in_specs=[row(d), pl.BlockSpec((1, d), lambda i: (0, 0)),
                  pl.BlockSpec(w_pad.shape, lambda i: (0, 0))],
        out_specs=out_specs, out_shape=out_shape,
        compiler_params=pltpu.CompilerParams(dimension_semantics=("arbitrary",),
                                             vmem_limit_bytes=VMEM_LIMIT),
        name="inproj",
    )(x2, norm_w.reshape(1, d), w_pad)


def _bias_kernel(tab_ref, out_ref):
    h = pl.program_id(0)
    t = lax.broadcasted_iota(I32, (KV_TILE, KV_TILE), 0)
    s = lax.broadcasted_iota(I32, (KV_TILE, KV_TILE), 1)
    far = tab_ref[N_BUCKETS // 2 - 1, h]
    for kind in (0, 1):
        rel = s - t - KV_TILE * kind
        n = jnp.abs(rel)
        large = jnp.full_like(n, N_BUCKETS // 4)
        for thr in _T5_THRESHOLDS:
            large = large + (n >= thr).astype(I32)
        bucket = jnp.where(rel > 0, N_BUCKETS // 2, 0) + jnp.where(n < N_BUCKETS // 4, n, large)
        bias = jnp.zeros((KV_TILE, KV_TILE), F32)
        for b in range(N_BUCKETS):
            bias = jnp.where(bucket == b, tab_ref[b, h], bias)
        bias = bias - far
        if kind == 0:
            blocked = jnp.logical_and(h < A_HEADS, (s // CHUNK) > (t // CHUNK))
            bias = jnp.where(blocked, NEG_INF, bias)
        out_ref[kind] = bias


def _bias_tiles(rel_bias):
    nh = rel_bias.shape[1]
    return pl.pallas_call(
        _bias_kernel, grid=(nh,),
        in_specs=[pl.BlockSpec(memory_space=pltpu.SMEM)],
        out_specs=pl.BlockSpec((None, 2, KV_TILE, KV_TILE), lambda h: (h, 0, 0, 0)),
        out_shape=jax.ShapeDtypeStruct((nh, 2, KV_TILE, KV_TILE), F32),
        name="bias_tiles",
    )(rel_bias)


def _stack_masked_queries(q, qm_scr):
    low = lax.broadcasted_iota(I32, (1, LANES), 1) < LANES // 2
    for p in range(N_UNITS // 2):
        qp = q[:, p * LANES:(p + 1) * LANES]
        zero = jnp.zeros_like(qp)
        qm_scr[p, 0:ROWS, 0:LANES] = jnp.where(low, qp, zero)
        qm_scr[p, ROWS:2 * ROWS, 0:LANES] = jnp.where(low, zero, qp)
    if qm_scr.shape[-1] == 2 * LANES:
        hit = (lax.broadcasted_iota(I32, (ROWS, LANES), 0)
               == lax.broadcasted_iota(I32, (ROWS, LANES), 1))
        eye = jnp.where(hit, MASKED_LOGIT, 0.0).astype(BF16)
        for p in range(N_UNITS // 2):
            qm_scr[p, 0:ROWS, LANES:2 * LANES] = eye
            qm_scr[p, ROWS:2 * ROWS, LANES:2 * LANES] = eye


def _two_pass_attention(i, r0, qm_scr, k_ref, v_ref, bias_ref, bias_head, unsel_scr,
                        s_scr, mx_scr, l_scr, acc_scr):
    n_tiles = i + 1
    n_far = jnp.maximum(i - 1, 0)
    mx_scr[...] = jnp.full(mx_scr.shape, NEG_INF, F32)
    l_scr[...] = jnp.zeros(l_scr.shape, F32)
    acc_scr[...] = jnp.zeros(acc_scr.shape, F32)

    def lane_groups(x):
        return [x[:, c * LANES:(c + 1) * LANES] for c in range(x.shape[1] // LANES)]

    def run_tiles(count, fn):
        def body(t, carry):
            fn(8 * t, 8 * KV_TILE)
            return carry
        lax.fori_loop(0, count // 8, body, 0)

        @pl.when(count % 8 >= 4)
        def _():
            fn(8 * (count // 8), 4 * KV_TILE)

        @pl.when(count % 4 >= 2)
        def _():
            fn(4 * (count // 4), 2 * KV_TILE)

        @pl.when(count % 2 == 1)
        def _():
            fn(count - 1, KV_TILE)

    def logits_tile(j, width, bias_kinds=()):
        start = pl.multiple_of(j * KV_TILE, KV_TILE)
        for p in range(N_UNITS // 2):
            kc = k_ref[pl.ds(start, width), p * LANES:(p + 1) * LANES]
            if unsel_scr is not None:
                kc = jnp.concatenate([kc, unsel_scr[pl.ds(start, width), :]], axis=1)
            s2 = _dot_nt(qm_scr[p], kc)
            for half in range(2):
                u = 2 * p + half
                s = s2[half * ROWS:(half + 1) * ROWS]
                if bias_kinds:
                    s = s + jnp.concatenate(
                        [bias_ref[bias_head(u), kind, pl.ds(r0, ROWS), :] for kind in bias_kinds],
                        axis=1)[:, :width]
                s_scr[u, :, pl.ds(start, width)] = s
                mx_scr[u] = jnp.maximum(mx_scr[u], functools.reduce(jnp.maximum, lane_groups(s)))

    run_tiles(n_far, logits_tile)

    upper = r0 != 0
    lower = r0 == 0

    @pl.when(jnp.logical_and(i >= 1, upper))
    def _():
        logits_tile(i - 1, 2 * KV_TILE, (1, 0))

    @pl.when(jnp.logical_and(i >= 1, lower))
    def _():
        logits_tile(i - 1, 2 * KV_TILE - LANES, (1, 0))

    @pl.when(jnp.logical_and(i == 0, upper))
    def _():
        logits_tile(0, KV_TILE, (0,))

    @pl.when(jnp.logical_and(i == 0, lower))
    def _():
        logits_tile(0, KV_TILE - LANES, (0,))

    for u in range(N_UNITS):
        m = jnp.max(mx_scr[u], axis=1, keepdims=True)
        mx_scr[u] = jnp.broadcast_to(m, (ROWS, LANES))

    def exp_tile(j, width):
        start = pl.multiple_of(j * KV_TILE, KV_TILE)
        for u in range(N_UNITS):
            m = mx_scr[u]
            ps = [jnp.exp(sg - m) for sg in lane_groups(s_scr[u, :, pl.ds(start, width)])]
            pb = jnp.concatenate(ps, axis=1).astype(BF16)
            g = u // 2
            vc = v_ref[pl.ds(start, width), g * LANES:(g + 1) * LANES]
            both = jnp.dot(pb, jnp.concatenate([vc, jnp.ones_like(vc)], axis=1),
                           preferred_element_type=F32)
            acc_scr[u] = acc_scr[u] + both[:, :LANES]
            l_scr[u] = l_scr[u] + both[:, LANES:]

    @pl.when(upper)
    def _():
        run_tiles(n_tiles, exp_tile)

    @pl.when(lower)
    def _():
        run_tiles(i, exp_tile)
        exp_tile(i, KV_TILE - LANES)


def _normalised(u, l_scr, acc_scr):
    return acc_scr[u] / l_scr[u]


def _tree_sum(x):
    parts = [x[t] for t in range(x.shape[0])]
    while len(parts) > 1:
        parts = [parts[t] + parts[t + 1] for t in range(0, len(parts) - 1, 2)] + (
            [parts[-1]] if len(parts) % 2 else [])
    return parts[0]


def _silu(g):
    return g * (1.0 / (1.0 + jnp.exp(-g)))


def _attention_scratch(s, q_width=LANES):
    return [pltpu.VMEM((N_UNITS // 2, 2 * ROWS, q_width), BF16),
            pltpu.VMEM((N_UNITS, ROWS, s), F32),
            pltpu.VMEM((N_UNITS, ROWS, LANES), F32),
            pltpu.VMEM((N_UNITS, ROWS, LANES), F32),
            pltpu.VMEM((N_UNITS, ROWS, LANES), F32)]


def _diff_kernel(lq1, lk1, lq2, lk2, subw_ref, q_ref, k_ref, v_ref, g_ref, bias_ref, o_ref,
                 qm_scr, s_scr, mx_scr, l_scr, acc_scr, *, lambda_init):
    tb = pl.program_id(1)
    i = tb // 2
    r0 = pl.multiple_of((tb % 2) * ROWS, ROWS)
    lam = (jnp.exp(jnp.sum(lq1[...] * lk1[...], axis=-1, keepdims=True))
           - jnp.exp(jnp.sum(lq2[...] * lk2[...], axis=-1, keepdims=True)) + lambda_init)
    _stack_masked_queries(q_ref[...], qm_scr)
    _two_pass_attention(i, r0, qm_scr, k_ref, v_ref, bias_ref, lambda u: u // 2, None,
                        s_scr, mx_scr, l_scr, acc_scr)
    for h in range(A_HEADS):
        o = _normalised(2 * h, l_scr, acc_scr) - lam * _normalised(2 * h + 1, l_scr, acc_scr)
        y = (o * lax.rsqrt(jnp.mean(o * o, axis=-1, keepdims=True) + EPS)) * subw_ref[...]
        y = y * (1.0 - lambda_init)
        cols = slice(h * LANES, (h + 1) * LANES)
        o_ref[:, cols] = (y * _silu(g_ref[:, cols])).astype(BF16)


def _diff_attention(aq, ak, av, ag, bias, lq1, lk1, lq2, lk2, subln_w, lambda_init):
    b, s, _ = aq.shape
    grid = (b, s // ROWS)
    vec = lambda a: pl.BlockSpec((1, a.shape[-1]), lambda bi, t: (0, 0))
    rows = pl.BlockSpec((None, ROWS, HEAD_W), lambda bi, t: (bi, t, 0))
    full = _resident((None, s, HEAD_W), lambda bi, t: (bi, 0, 0))
    kern = functools.partial(_diff_kernel, lambda_init=lambda_init)
    return pl.pallas_call(
        kern, grid=grid,
        in_specs=[vec(lq1), vec(lk1), vec(lq2), vec(lk2), vec(subln_w), rows, full, full, rows,
                  _resident(bias.shape, lambda bi, t: (0, 0, 0, 0))],
        out_specs=rows,
        out_shape=jax.ShapeDtypeStruct((b, s, HEAD_W), BF16),
        scratch_shapes=_attention_scratch(s),
        compiler_params=pltpu.CompilerParams(dimension_semantics=("arbitrary",) * 2,
                                             vmem_limit_bytes=VMEM_LIMIT),
        name="diff_attention",
    )(lq1, lk1, lq2, lk2, subln_w, aq, ak, av, ag, bias)


def _dsa_kernel(iq_ref, ik_ref, iw_ref, q_ref, k_ref, v_ref, g_ref, bias_ref, o_ref,
                sc_scr, iqm_scr, unsel_scr, qm_scr, s_scr, mx_scr, l_scr, acc_scr, *, k_sel):
    tb = pl.program_id(1)
    i = tb // 2
    n_tiles = i + 1
    n_scan = (n_tiles + 1) // 2
    r0 = pl.multiple_of((tb % 2) * ROWS, ROWS)
    lane = lax.broadcasted_iota(I32, (1, LANES), 1)
    low = lane < B_HEAD_DIM
    q_chunk = (tb * ROWS + lane) // CHUNK
    key_in_tile = lax.broadcasted_iota(I32, (KV_TILE, LANES), 0)

    iq = iq_ref[...]
    for hh in range(IDX_HEADS):
        pair = iq[:, (hh // 2) * LANES:(hh // 2 + 1) * LANES]
        iqm_scr[hh * ROWS:(hh + 1) * ROWS, :] = jnp.where(low == (hh % 2 == 0), pair,
                                                          jnp.zeros_like(pair))
    iw_t = iw_ref[...].T

    def score_tile(j, diagonal):
        start = pl.multiple_of(j * KV_TILE, KV_TILE)
        logits = _dot_nt(ik_ref[pl.ds(start, KV_TILE), :], iqm_scr[...])
        sc = jnp.zeros((KV_TILE, LANES), F32)
        for hh in range(IDX_HEADS):
            sc = sc + iw_t[hh:hh + 1, :] * jnp.maximum(logits[:, hh * ROWS:(hh + 1) * ROWS], 0.0)
        if diagonal:
            sc = jnp.where(((start + key_in_tile) // CHUNK) <= q_chunk, sc, NEG_INF)
        sc_scr[pl.ds(start, KV_TILE), :] = sc

    def far_scores(t, carry):
        for d in range(4):
            score_tile(4 * t + d, False)
        return carry

    lax.fori_loop(0, i // 4, far_scores, 0)

    @pl.when((i // 2) % 2 == 1)
    def _():
        score_tile(4 * (i // 4), False)
        score_tile(4 * (i // 4) + 1, False)

    @pl.when(i % 2 == 1)
    def _():
        score_tile(i - 1, False)
        score_tile(i, True)

    @pl.when(i % 2 == 0)
    def _():
        score_tile(i, True)

    @pl.when(n_tiles % 2 == 1)
    def _():
        pad = pl.ds(pl.multiple_of(n_tiles * KV_TILE, KV_TILE), KV_TILE)
        sc_scr[pad, :] = jnp.full((KV_TILE, LANES), NEG_INF, F32)

    def key_to_float(key):
        bits = key ^ (lax.shift_right_arithmetic(key, 31) & 0x7FFFFFFF)
        return lax.bitcast_convert_type(bits, F32)

    def scan_count(pred):
        def body(t, accs):
            blk = sc_scr[pl.ds(pl.multiple_of(t * SCAN_TILE, SCAN_TILE), SCAN_TILE), :]
            hit = pred(blk.reshape(SCAN_TILE // 8, 8, LANES))
            accs = list(accs)
            for v in range(SCAN_TILE // 8):
                c = v % len(accs)
                accs[c] = jnp.where(hit[v], accs[c] + 1.0, accs[c])
            return tuple(accs)
        zero = jnp.zeros((8, LANES), F32)
        accs = lax.fori_loop(0, n_scan, body, (zero,) * SCAN_CHAINS)
        return jnp.sum(functools.reduce(lambda a, b: a + b, accs), axis=0, keepdims=True)

    kf = float(k_sel)

    def bit_step(b, carry):
        key, c_key = carry
        cand = key + lax.shift_left(jnp.int32(1), 31 - b)
        cand_f = key_to_float(cand)
        c = scan_count(lambda blk: blk >= cand_f)
        ok = jnp.logical_or(c >= kf, cand <= KEY_NEG_INF)
        return jnp.where(ok, cand, key), jnp.where(ok, c, c_key)

    n_keys = jnp.full((1, LANES), 1.0, F32) * (n_tiles * KV_TILE).astype(F32)
    thr_key, c_thr = lax.fori_loop(0, 32, bit_step,
                                   (jnp.full((1, LANES), INT_MIN, I32), n_keys))
    thr = key_to_float(thr_key)

    def tile_count(pred):
        def body(j, acc):
            start = pl.multiple_of(j * KV_TILE, KV_TILE)
            hit = jnp.where(pred(sc_scr[pl.ds(start, KV_TILE), :], start + key_in_tile), 1.0, 0.0)
            return acc + _tree_sum(hit.reshape(KV_TILE // 8, 8, LANES))
        acc = lax.fori_loop(0, n_tiles, body, jnp.zeros((8, LANES), F32))
        return jnp.sum(acc, axis=0, keepdims=True)

    def tie_limit():
        need = kf - tile_count(lambda kc, idx: kc > thr)

        def idx_step(b, lim):
            cand = lim + lax.shift_left(jnp.int32(1), 11 - b)
            f = tile_count(lambda kc, idx: jnp.logical_and(kc == thr, idx < cand))
            return jnp.where(f < need, cand, lim)

        return lax.fori_loop(0, 12, idx_step, jnp.zeros((1, LANES), I32))

    lim = lax.cond(jnp.max(c_thr) > kf, tie_limit,
                   lambda: jnp.full((1, LANES), 2 ** 30, I32))

    def mask_tile(j, diagonal):
        start = pl.multiple_of(j * KV_TILE, KV_TILE)
        kc = sc_scr[pl.ds(start, KV_TILE), :]
        idx = start + key_in_tile
        sel = jnp.logical_or(kc > thr, jnp.logical_and(kc == thr, idx <= lim))
        if diagonal:
            sel = jnp.logical_and(sel, (idx // CHUNK) <= q_chunk)
        unsel_scr[pl.ds(start, KV_TILE), :] = jnp.where(sel, 0.0, 1.0).astype(BF16)

    def far_mask(j, carry):
        mask_tile(j, False)
        return carry

    lax.fori_loop(0, i, far_mask, 0)
    mask_tile(i, True)

    _stack_masked_queries(q_ref[...], qm_scr)
    _two_pass_attention(i, r0, qm_scr, k_ref, v_ref, bias_ref, lambda u: u, unsel_scr,
                        s_scr, mx_scr, l_scr, acc_scr)
    for p in range(B_HEADS // 2):
        o = jnp.where(low, _normalised(2 * p, l_scr, acc_scr), _normalised(2 * p + 1, l_scr, acc_scr))
        cols = slice(p * LANES, (p + 1) * LANES)
        o_ref[:, cols] = (o * _silu(g_ref[:, cols])).astype(BF16)


def _dsa_attention(iq, ik, iw, bq, bk, bv, bg, bias, k_sel):
    b, s, _ = bq.shape
    grid = (b, s // ROWS)
    rows = lambda w: pl.BlockSpec((None, ROWS, w), lambda bi, t: (bi, t, 0))
    full = lambda w: _resident((None, s, w), lambda bi, t: (bi, 0, 0))
    kern = functools.partial(_dsa_kernel, k_sel=k_sel)
    return pl.pallas_call(
        kern, grid=grid,
        in_specs=[rows(256), full(LANES), rows(LANES), rows(HEAD_W), full(HEAD_W), full(HEAD_W),
                  rows(HEAD_W), _resident(bias.shape, lambda bi, t: (0, 0, 0, 0))],
        out_specs=rows(HEAD_W),
        out_shape=jax.ShapeDtypeStruct((b, s, HEAD_W), BF16),
        scratch_shapes=[pltpu.VMEM((s, LANES), F32),
                        pltpu.VMEM((IDX_HEADS * ROWS, LANES), BF16),
                        pltpu.VMEM((s, LANES), BF16)]
        + _attention_scratch(s, 2 * LANES),
        compiler_params=pltpu.CompilerParams(dimension_semantics=("arbitrary",) * 2,
                                             vmem_limit_bytes=VMEM_LIMIT),
        name="dsa_attention",
    )(iq, ik, iw, bq, bk, bv, bg, bias)


def _outproj_kernel(ao_ref, bo_ref, x_ref, w_ref, fw_ref, o_ref):
    y = jnp.dot(ao_ref[...], w_ref[0:HEAD_W, :], preferred_element_type=F32)
    y = y + jnp.dot(bo_ref[...], w_ref[HEAD_W:2 * HEAD_W, :], preferred_element_type=F32)
    z = x_ref[...] + y
    o_ref[...] = (z * lax.rsqrt(jnp.mean(z * z, axis=-1, keepdims=True) + EPS)) * fw_ref[...]


def _outproj(ao, bo, x2, w_out, final_w):
    n, d = x2.shape
    row = lambda w: pl.BlockSpec((PROJ_TM, w), lambda i: (i, 0))
    return pl.pallas_call(
        _outproj_kernel, grid=(n // PROJ_TM,),
        in_specs=[row(HEAD_W), row(HEAD_W), row(d), pl.BlockSpec(w_out.shape, lambda i: (0, 0)),
                  pl.BlockSpec((1, d), lambda i: (0, 0))],
        out_specs=row(d), out_shape=jax.ShapeDtypeStruct((n, d), F32),
        compiler_params=pltpu.CompilerParams(dimension_semantics=("arbitrary",),
                                             vmem_limit_bytes=VMEM_LIMIT),
        name="outproj",
    )(ao, bo, x2, w_out, final_w.reshape(1, d))


def kernel(x, norm_w, w_in, w_out, lambda_q1, lambda_k1, lambda_q2, lambda_k2, subln_w, rel_bias,
           final_norm_w):
    b, s, d = x.shape
    depth = norm_w.shape[0]
    assert depth == 1, "single-layer trunk"
    assert s % KV_TILE == 0 and (b * s) % PROJ_TM == 0
    layer = 0
    lambda_init = 0.8 - 0.6 * math.exp(-0.3 * layer)
    k_sel = min(TOPK_MAX, s // 4)

    w = w_in[layer]
    main = 8 * HEAD_W
    iq_w = w[:, main:main + IDX_HEADS * IDX_DIM]
    ik_w = w[:, main + IDX_HEADS * IDX_DIM:main + IDX_HEADS * IDX_DIM + IDX_DIM]
    iw_w = w[:, main + IDX_HEADS * IDX_DIM + IDX_DIM:]
    pad = jnp.zeros((d, LANES - IDX_HEADS), w.dtype)
    w_pad = jnp.concatenate([w[:, :main], iq_w, ik_w, ik_w, iw_w, pad], axis=1).astype(BF16)

    x2 = x.reshape(b * s, d)
    aq, ak, av, ag, bq, bk, bv, bg, iq, ik, iw = _inproj(x2, norm_w[layer], w_pad)
    r3 = lambda a: a.reshape(b, s, a.shape[-1])
    bias = _bias_tiles(rel_bias)

    vec = lambda a: a.reshape(1, -1)
    ao = _diff_attention(r3(aq), r3(ak), r3(av), r3(ag), bias[:A_HEADS],
                         vec(lambda_q1[layer]), vec(lambda_k1[layer]),
                         vec(lambda_q2[layer]), vec(lambda_k2[layer]), vec(subln_w[layer]),
                         lambda_init)
    bo = _dsa_attention(r3(iq), r3(ik), r3(iw), r3(bq), r3(bk), r3(bv), r3(bg),
                        bias[A_HEADS:], k_sel)

    out = _outproj(ao.reshape(b * s, -1), bo.reshape(b * s, -1), x2,
                   w_out[layer].astype(BF16), final_norm_w)
    return out.reshape(b, s, d)
```
